```python
import jax, jax.numpy as jnp
from jax import lax
import numpy as np

D_MODEL = 1024
BATCH = 8
SEQ = 2048
DEPTH = 4
DEC_BATCH = 128
DEC_SEQ = 1
PAST_LEN = 2048
PAGE_SIZE = 128

POOL_WINDOWS = (2, 4, 8, 16)
POOL_GROUPS = len(POOL_WINDOWS)
POOL_W = D_MODEL // 4
POOL_GC = POOL_W // POOL_GROUPS
POOL_HIST = max(POOL_WINDOWS) - 1
SB_HEADS = 8
SB_HEAD_DIM = 64
SB_W = SB_HEADS * SB_HEAD_DIM
SB_BLOCK = 128
SB_BIAS_INIT = -7.0
GDN_HEADS = 4
GDN_DK = 64
GDN_DV = 64
GDN_QK_W = GDN_HEADS * GDN_DK
GDN_V_W = GDN_HEADS * GDN_DV
GDN_QKV_W = 2 * GDN_QK_W + GDN_V_W
GDN_CONV = 4
GDN_CHUNK = 64
N_BRANCH = 3
MIX_W = POOL_W + SB_W + GDN_V_W
D_FF = -(-(8 * D_MODEL) // (3 * 256)) * 256
RMS_EPS = 1e-6
OFF_POOL = 0
OFF_SBQ = OFF_POOL + POOL_W
OFF_SBK = OFF_SBQ + SB_W
OFF_SBV = OFF_SBK + SB_W
OFF_GQKV = OFF_SBV + SB_W
OFF_GA = OFF_GQKV + GDN_QKV_W
OFF_GB = OFF_GA + GDN_HEADS
OFF_GZ = OFF_GB + GDN_HEADS
OFF_GATE = OFF_GZ + GDN_V_W
IN_W = OFF_GATE + N_BRANCH * D_MODEL

kernel_name = 'hybrid_pool_stickbreak_gdn_step'


def rmsnorm(x, g):
    xf = x.astype(jnp.float32)
    y = xf * lax.rsqrt(jnp.mean(xf * xf, axis=-1, keepdims=True) + RMS_EPS)
    return (y * g.astype(jnp.float32)).astype(x.dtype)


def l2norm(x):
    xf = x.astype(jnp.float32)
    return xf * lax.rsqrt(jnp.sum(xf * xf, axis=-1, keepdims=True) + 1e-6)


def pool_mix(u, hist, pos0, w_grp, scale):
    B, L, C = u.shape
    P = hist.shape[1]
    ext = jnp.concatenate([hist.astype(u.dtype), u], axis=1)
    cs = jnp.concatenate([jnp.zeros((B, 1, C), jnp.float32),
                          jnp.cumsum(ext.astype(jnp.float32), axis=1)], axis=1)
    t_ext = np.arange(L) + P
    abs_pos = np.arange(L) + pos0
    end = cs[:, t_ext + 1]
    uf = u.astype(jnp.float32)
    parts = []
    for gi, win in enumerate(POOL_WINDOWS):
        sl = slice(gi * POOL_GC, (gi + 1) * POOL_GC)
        start = np.maximum(t_ext + 1 - win, 0)
        cnt = jnp.asarray(np.minimum(win, abs_pos + 1).astype(np.float32))
        mean = (end[..., sl] - cs[:, start, sl]) / cnt[None, :, None]
        parts.append(mean - uf[..., sl])
    pooled = jnp.concatenate(parts, axis=-1).reshape(B, L, POOL_GROUPS, POOL_GC)
    mixed = jnp.einsum('blgc,gcd->blgd', pooled, w_grp.astype(jnp.float32)).reshape(B, L, C)
    out = (mixed * scale.astype(jnp.float32)).astype(u.dtype)
    return out, ext[:, ext.shape[1] - POOL_HIST:]


def stick_breaking(q, k, v, pos0, bias):
    Lq = q.shape[1]
    qf = q.astype(jnp.float32) * (SB_HEAD_DIM ** -0.5)
    kf = k.astype(jnp.float32)
    vf = v.astype(jnp.float32)
    bf = bias.astype(jnp.float32)[None, :, None, None]
    outs = []
    for qs in range(0, Lq, SB_BLOCK):
        qe = min(qs + SB_BLOCK, Lq)
        kend = pos0 + qe
        z = jnp.einsum('bqhd,bkhd->bhqk', qf[:, qs:qe], kf[:, :kend]) + bf
        vis = np.arange(kend)[None, :] < (pos0 + np.arange(qs, qe))[:, None]
        log_1m = jnp.where(vis, jax.nn.log_sigmoid(-z), 0.0)
        suffix = lax.cumsum(log_1m, axis=3, reverse=True) - log_1m
        a = jnp.where(vis, jnp.exp(jax.nn.log_sigmoid(z) + suffix), 0.0)
        outs.append(jnp.einsum('bhqk,bkhd->bqhd', a, vf[:, :kend]))
    return jnp.concatenate(outs, axis=1)


def short_conv(u, hist, w):
    L = u.shape[1]
    ext = jnp.concatenate([hist.astype(u.dtype), u], axis=1)
    out = ext[:, 0:L] * w[0]
    for i in range(1, GDN_CONV):
        out = out + ext[:, i:i + L] * w[i]
    return jax.nn.silu(out), ext[:, ext.shape[1] - (GDN_CONV - 1):]


def gdn_chunked(q, k, v, g, beta, s0):
    B, L, H, _ = q.shape
    n = L // GDN_CHUNK
    C = GDN_CHUNK

    def blk(x):
        return x.reshape(B, n, C, H, x.shape[-1]).transpose(0, 3, 1, 2, 4)

    qc, kc, vc = blk(q), blk(k), blk(v)
    gc = jnp.cumsum(g.reshape(B, n, C, H).transpose(0, 3, 1, 2), axis=-1)
    bc = beta.reshape(B, n, C, H).transpose(0, 3, 1, 2)[..., None]
    incl = jnp.tril(jnp.ones((C, C), bool))
    strict = jnp.tril(jnp.ones((C, C), bool), -1)
    diff = gc[..., :, None] - gc[..., None, :]
    decay = jnp.where(incl, jnp.exp(jnp.where(incl, diff, 0.0)), 0.0)
    kb = kc * bc
    a_mat = jnp.where(strict, jnp.einsum('bhnid,bhnjd->bhnij', kb, kc) * decay, 0.0) + jnp.eye(C, dtype=jnp.float32)
    u = lax.linalg.triangular_solve(a_mat, vc * bc, left_side=True, lower=True, unit_diagonal=True)
    w = lax.linalg.triangular_solve(a_mat, kb * jnp.exp(gc)[..., None], left_side=True, lower=True, unit_diagonal=True)
    qk = jnp.where(incl, jnp.einsum('bhnid,bhnjd->bhnij', qc, kc) * decay, 0.0)
    q_dec = qc * jnp.exp(gc)[..., None]
    k_dec = kc * jnp.exp(gc[..., -1:] - gc)[..., None]
    g_last = jnp.exp(gc[..., -1])

    def step(s, xs):
        u_i, w_i, qk_i, qd_i, kd_i, gl_i = xs
        v_new = u_i - jnp.einsum('bhck,bhkv->bhcv', w_i, s)
        o = jnp.einsum('bhck,bhkv->bhcv', qd_i, s) + jnp.einsum('bhcj,bhjv->bhcv', qk_i, v_new)
        s = s * gl_i[..., None, None] + jnp.einsum('bhck,bhcv->bhkv', kd_i, v_new)
        return s, o

    xs = tuple(jnp.moveaxis(t, 2, 0) for t in (u, w, qk, q_dec, k_dec, g_last))
    s, o = lax.scan(step, s0.astype(jnp.float32), xs)
    return o.transpose(1, 0, 3, 2, 4).reshape(B, L, H, GDN_DV), s


def gdn_recurrent(q, k, v, g, beta, s0):
    xs = tuple(jnp.moveaxis(t, 1, 0) for t in (q, k, v, g, beta))

    def step(s, xt):
        q_t, k_t, v_t, g_t, b_t = xt
        s = s * jnp.exp(g_t)[..., None, None]
        delta = (v_t - jnp.einsum('bhk,bhkv->bhv', k_t, s)) * b_t[..., None]
        s = s + jnp.einsum('bhk,bhv->bhkv', k_t, delta)
        return s, jnp.einsum('bhk,bhkv->bhv', q_t, s)

    s, o = lax.scan(step, s0.astype(jnp.float32), xs)
    return jnp.moveaxis(o, 0, 1), s


def mixer(h, pos0, pool_hist, conv_hist, s0, k_past, v_past, chunked,
          w_in_l, pool_w_l, pool_scale_l, sb_bias_l, conv_w_l, a_log_l, dt_bias_l, gnorm_l, w_branch_l, w_o_l):
    B, L, _ = h.shape
    f32 = jnp.float32
    p = h @ w_in_l
    o_pool, pool_new = pool_mix(p[..., OFF_POOL:OFF_SBQ], pool_hist, pos0, pool_w_l, pool_scale_l)
    q = p[..., OFF_SBQ:OFF_SBK].reshape(B, L, SB_HEADS, SB_HEAD_DIM)
    k = p[..., OFF_SBK:OFF_SBV].reshape(B, L, SB_HEADS, SB_HEAD_DIM)
    v = p[..., OFF_SBV:OFF_GQKV].reshape(B, L, SB_HEADS, SB_HEAD_DIM)
    k_all = jnp.concatenate([k_past.astype(k.dtype), k], axis=1)
    v_all = jnp.concatenate([v_past.astype(v.dtype), v], axis=1)
    o_sb = stick_breaking(q, k_all, v_all, pos0, sb_bias_l).reshape(B, L, SB_W).astype(h.dtype)
    qkv, conv_new = short_conv(p[..., OFF_GQKV:OFF_GA], conv_hist, conv_w_l)
    gq = l2norm(qkv[..., :GDN_QK_W].reshape(B, L, GDN_HEADS, GDN_DK)) * (GDN_DK ** -0.5)
    gk = l2norm(qkv[..., GDN_QK_W:2 * GDN_QK_W].reshape(B, L, GDN_HEADS, GDN_DK))
    gv = qkv[..., 2 * GDN_QK_W:].reshape(B, L, GDN_HEADS, GDN_DV).astype(f32)
    a_in = p[..., OFF_GA:OFF_GB].astype(f32)
    b_in = p[..., OFF_GB:OFF_GZ].astype(f32)
    g = -jnp.exp(a_log_l.astype(f32)) * jax.nn.softplus(a_in + dt_bias_l.astype(f32))
    beta = jax.nn.sigmoid(b_in)
    if chunked:
        o_g, s_new = gdn_chunked(gq, gk, gv, g, beta, s0)
    else:
        o_g, s_new = gdn_recurrent(gq, gk, gv, g, beta, s0)
    z = p[..., OFF_GZ:OFF_GATE].reshape(B, L, GDN_HEADS, GDN_DV).astype(f32)
    o_g = (rmsnorm(o_g, gnorm_l) * jax.nn.silu(z)).reshape(B, L, GDN_V_W).astype(h.dtype)
    gates = jax.nn.sigmoid(p[..., OFF_GATE:].reshape(B, L, N_BRANCH, D_MODEL).astype(f32)).astype(h.dtype)
    br_pool = o_pool @ w_branch_l[:POOL_W]
    br_sb = o_sb @ w_branch_l[POOL_W:POOL_W + SB_W]
    br_gdn = o_g @ w_branch_l[POOL_W + SB_W:]
    m = gates[..., 0, :] * br_pool + gates[..., 1, :] * br_sb + gates[..., 2, :] * br_gdn
    return m @ w_o_l, pool_new, conv_new, s_new.astype(h.dtype), k, v


def swiglu(h, w_in_l, w_out_l):
    gu = h @ w_in_l
    return (jax.nn.silu(gu[..., :D_FF]) * gu[..., D_FF:]) @ w_out_l


def setup_inputs(seed: int = 0) -> dict:
    key = jax.random.key(seed)
    ks = jax.random.split(key, 24)
    f32 = jnp.float32
    n_pages = PAST_LEN // PAGE_SIZE
    n_used = DEC_BATCH * n_pages
    n_phys = n_used + max(1, n_used // 4)

    def nrm(k, shape, s):
        return jax.random.normal(k, shape, f32) * s

    page_table = jax.random.permutation(ks[4], n_phys)[:n_used].reshape(DEC_BATCH, n_pages).astype(jnp.int32)
    row_scale = jnp.concatenate([jnp.full((POOL_W,), POOL_W ** -0.5, f32),
                                 jnp.full((SB_W,), SB_W ** -0.5, f32),
                                 jnp.full((GDN_V_W,), GDN_V_W ** -0.5, f32)])
    dt = jnp.exp(jax.random.uniform(ks[14], (DEPTH, GDN_HEADS), f32, float(np.log(1e-3)), float(np.log(1e-1))))
    return {
        'x_prompt': nrm(ks[0], (BATCH, SEQ, D_MODEL), 1.0),
        'x_sample': nrm(ks[1], (DEC_BATCH, DEC_SEQ, D_MODEL), 1.0),
        'cache_sb_k': nrm(ks[2], (DEPTH, n_phys, PAGE_SIZE, SB_HEADS, SB_HEAD_DIM), 1.0),
        'cache_sb_v': nrm(ks[3], (DEPTH, n_phys, PAGE_SIZE, SB_HEADS, SB_HEAD_DIM), 1.0),
        'page_table': page_table,
        'state_pool': nrm(ks[5], (DEPTH, DEC_BATCH, POOL_HIST, POOL_W), 1.0),
        'state_gdn_conv': nrm(ks[6], (DEPTH, DEC_BATCH, GDN_CONV - 1, GDN_QKV_W), 1.0),
        'state_gdn': nrm(ks[7], (DEPTH, DEC_BATCH, GDN_HEADS, GDN_DK, GDN_DV), GDN_DK ** -0.5),
        'norm1_g': 1.0 + nrm(ks[8], (DEPTH, D_MODEL), 0.02),
        'w_in': nrm(ks[9], (DEPTH, D_MODEL, IN_W), D_MODEL ** -0.5),
        'pool_w': nrm(ks[10], (DEPTH, POOL_GROUPS, POOL_GC, POOL_GC), POOL_GC ** -0.5),
        'pool_scale': 1.0 + nrm(ks[11], (DEPTH, POOL_W), 0.1),
        'sb_bias': SB_BIAS_INIT + nrm(ks[22], (DEPTH, SB_HEADS), 0.5),
        'gdn_conv_w': nrm(ks[12], (DEPTH, GDN_CONV, GDN_QKV_W), GDN_CONV ** -0.5),
        'gdn_a_log': jnp.log(jax.random.uniform(ks[13], (DEPTH, GDN_HEADS), f32, 1.0, 16.0)),
        'gdn_dt_bias': dt + jnp.log(-jnp.expm1(-dt)),
        'gdn_norm_g': 1.0 + nrm(ks[15], (DEPTH, GDN_DV), 0.02),
        'w_branch': jax.random.normal(ks[16], (DEPTH, MIX_W, D_MODEL), f32) * row_scale[None, :, None],
        'w_o': nrm(ks[17], (DEPTH, D_MODEL, D_MODEL), D_MODEL ** -0.5),
        'norm2_g': 1.0 + nrm(ks[18], (DEPTH, D_MODEL), 0.02),
        'w_ffn_in': nrm(ks[19], (DEPTH, D_MODEL, 2 * D_FF), D_MODEL ** -0.5),
        'w_ffn_out': nrm(ks[20], (DEPTH, D_FF, D_MODEL), D_FF ** -0.5),
        'normf_g': 1.0 + nrm(ks[21], (D_MODEL,), 0.02),
    }


def reference(x_prompt, x_sample, cache_sb_k, cache_sb_v, page_table, state_pool, state_gdn_conv, state_gdn,
              norm1_g, w_in, pool_w, pool_scale, sb_bias, gdn_conv_w, gdn_a_log, gdn_dt_bias, gdn_norm_g,
              w_branch, w_o, norm2_g, w_ffn_in, w_ffn_out, normf_g):
    bp = x_prompt.shape[0]
    bs = x_sample.shape[0]
    past_len = page_table.shape[1] * cache_sb_k.shape[2]
    pdt = x_prompt.dtype
    xp, xs = x_prompt, x_sample
    kp_l, vp_l, poolp_l, convp_l, sp_l = [], [], [], [], []
    ks_l, vs_l, pools_l, convs_l, ss_l = [], [], [], [], []
    for l in range(DEPTH):
        lw = (w_in[l], pool_w[l], pool_scale[l], sb_bias[l], gdn_conv_w[l], gdn_a_log[l], gdn_dt_bias[l],
              gdn_norm_g[l], w_branch[l], w_o[l])
        mp, pool_np, conv_np, s_np, k_np, v_np = mixer(
            rmsnorm(xp, norm1_g[l]), 0,
            jnp.zeros((bp, 0, POOL_W), pdt), jnp.zeros((bp, GDN_CONV - 1, GDN_QKV_W), pdt),
            jnp.zeros((bp, GDN_HEADS, GDN_DK, GDN_DV), jnp.float32),
            jnp.zeros((bp, 0, SB_HEADS, SB_HEAD_DIM), pdt), jnp.zeros((bp, 0, SB_HEADS, SB_HEAD_DIM), pdt),
            True, *lw)
        xp = xp + mp
        xp = xp + swiglu(rmsnorm(xp, norm2_g[l]), w_ffn_in[l], w_ffn_out[l])
        kp_l.append(k_np); vp_l.append(v_np); poolp_l.append(pool_np); convp_l.append(conv_np); sp_l.append(s_np)
        k_past = cache_sb_k[l][page_table].reshape(bs, past_len, SB_HEADS, SB_HEAD_DIM)
        v_past = cache_sb_v[l][page_table].reshape(bs, past_len, SB_HEADS, SB_HEAD_DIM)
        ms, pool_ns, conv_ns, s_ns, k_ns, v_ns = mixer(
            rmsnorm(xs, norm1_g[l]), past_len, state_pool[l], state_gdn_conv[l], state_gdn[l],
            k_past, v_past, False, *lw)
        xs = xs + ms
        xs = xs + swiglu(rmsnorm(xs, norm2_g[l]), w_ffn_in[l], w_ffn_out[l])
        ks_l.append(k_ns); vs_l.append(v_ns); pools_l.append(pool_ns); convs_l.append(conv_ns); ss_l.append(s_ns)
    y_prompt = rmsnorm(xp, normf_g)
    y_sample = rmsnorm(xs, normf_g)
    return (y_prompt, y_sample,
            jnp.stack(kp_l), jnp.stack(vp_l), jnp.stack(poolp_l), jnp.stack(convp_l), jnp.stack(sp_l),
            jnp.stack(ks_l), jnp.stack(vs_l), jnp.stack(pools_l), jnp.stack(convs_l), jnp.stack(ss_l))
```

```python
import functools

import numpy as np
import jax
import jax.numpy as jnp
from jax import lax
from jax.experimental import pallas as pl
from jax.experimental.pallas import tpu as pltpu

F32 = jnp.float32
BF16 = jnp.bfloat16

D_MODEL = 1024
POOL_WINDOWS = (2, 4, 8, 16)
POOL_W = 256
POOL_GC = 64
POOL_HIST = 15
SB_HEADS = 8
SB_HEAD_DIM = 64
SB_W = 512
GDN_HEADS = 4
GDN_DK = 64
GDN_DV = 64
GDN_QK_W = 256
GDN_V_W = 256
GDN_QKV_W = 768
GDN_CONV = 4
GDN_CHUNK = 64
D_FF = 2816
RMS_EPS = 1e-6
L2_EPS = 1e-6

R_POOL, R_SBQ, R_GQKV, R_GA, R_GZ, R_GATE, R_END = 0, 256, 1792, 2560, 2568, 2824, 5896
P_GQKV, P_POOL, P_SBQ, P_SBK, P_SBV, P_GZ, P_AB, P_GATE, P_W = 0, 768, 1024, 1536, 2048, 2560, 2816, 3072, 6144

LANES = 128
SB_TK = LANES
VMEM_LIMIT = 56 * 1024 * 1024


def _cparams(sem):
    return pltpu.CompilerParams(dimension_semantics=sem, vmem_limit_bytes=VMEM_LIMIT)


def _sigmoid(x):
    return 1.0 / (1.0 + jnp.exp(-x))


def _softplus(x):
    return jnp.maximum(x, 0.0) + jnp.log(1.0 + jnp.exp(-jnp.abs(x)))


def _split(x):
    hi = x.astype(BF16)
    lo = (x - hi.astype(F32)).astype(BF16)
    return hi, lo


def _dot(a, b):
    return jnp.dot(a, b, preferred_element_type=F32)


def _dot_nt(a, b):
    return lax.dot_general(a, b, (((1,), (1,)), ((), ())), preferred_element_type=F32)


def _dot_tn(a, b):
    return lax.dot_general(a, b, (((0,), (0,)), ((), ())), preferred_element_type=F32)


def _mm3(a, b):
    ah, al = _split(a)
    bh, bl = _split(b)
    return _dot(ah, bh) + (_dot(ah, bl) + _dot(al, bh))


def _dot_exact_rhs(x, m):
    hi, lo = _split(x)
    return _dot(hi, m) + _dot(lo, m)


def _rms_scale(x):
    return lax.rsqrt(jnp.mean(x * x, axis=-1, keepdims=True) + RMS_EPS)


def _rms_matmul_kernel(x_ref, g_ref, w_ref, o_ref, h_ref):
    @pl.when(pl.program_id(1) == 0)
    def _():
        x = x_ref[...]
        h_ref[...] = (x * _rms_scale(x) * g_ref[...]).astype(BF16)

    o_ref[...] = _dot(h_ref[...], w_ref[...])


def rms_matmul(x, g, w, *, tm, tn):
    t, d = x.shape
    n = w.shape[1]
    return pl.pallas_call(
        _rms_matmul_kernel,
        grid=(t // tm, n // tn),
        in_specs=[pl.BlockSpec((tm, d), lambda i, j: (i, 0)),
                  pl.BlockSpec((1, d), lambda i, j: (0, 0)),
                  pl.BlockSpec((d, tn), lambda i, j: (0, j))],
        out_specs=pl.BlockSpec((tm, tn), lambda i, j: (i, j)),
        out_shape=jax.ShapeDtypeStruct((t, n), F32),
        scratch_shapes=[pltpu.VMEM((tm, d), BF16)],
        compiler_params=_cparams(("parallel", "arbitrary")),
        name="in_proj",
    )(x, g.reshape(1, d), w)


def _ffn_kernel(x_ref, g_ref, wg_ref, wu_ref, wo_ref, gf_ref, o_ref, h_ref, acc_ref, *, final):
    f = pl.program_id(1)

    @pl.when(f == 0)
    def _():
        x = x_ref[...]
        h_ref[...] = (x * _rms_scale(x) * g_ref[...]).astype(BF16)
        acc_ref[...] = x

    h = h_ref[...]
    gate = _dot(h, wg_ref[...])
    up = _dot(h, wu_ref[...])
    act = (gate * _sigmoid(gate)) * up
    acc_ref[...] += _dot(act.astype(BF16), wo_ref[...])

    @pl.when(f == pl.num_programs(1) - 1)
    def _():
        y = acc_ref[...]
        if final:
            y = y * _rms_scale(y) * gf_ref[...]
        o_ref[...] = y


def ffn(x, g, w_in, w_out, gf, *, tm, tf, final):
    t, d = x.shape
    dff = w_out.shape[0]
    nf = dff // tf
    return pl.pallas_call(
        functools.partial(_ffn_kernel, final=final),
        grid=(t // tm, nf),
        in_specs=[pl.BlockSpec((tm, d), lambda i, f: (i, 0)),
                  pl.BlockSpec((1, d), lambda i, f: (0, 0)),
                  pl.BlockSpec((d, tf), lambda i, f: (0, f)),
                  pl.BlockSpec((d, tf), lambda i, f: (0, f + nf)),
                  pl.BlockSpec((tf, d), lambda i, f: (f, 0)),
                  pl.BlockSpec((1, d), lambda i, f: (0, 0))],
        out_specs=pl.BlockSpec((tm, d), lambda i, f: (i, 0)),
        out_shape=jax.ShapeDtypeStruct((t, d), F32),
        scratch_shapes=[pltpu.VMEM((tm, d), BF16), pltpu.VMEM((tm, d), F32)],
        compiler_params=_cparams(("parallel", "arbitrary")),
        name="ffn",
    )(x, g.reshape(1, d), w_in, w_in, w_out, gf.reshape(1, d))


def _merge_kernel(x_ref, op_ref, osb_ref, og_ref, g0_ref, g1_ref, g2_ref, wb_ref, wo_ref, o_ref):
    br_pool = _dot(op_ref[...].astype(BF16), wb_ref[0:POOL_W, :])
    br_sb = _dot(osb_ref[...].astype(BF16), wb_ref[POOL_W:POOL_W + SB_W, :])
    br_gdn = _dot(og_ref[...].astype(BF16), wb_ref[POOL_W + SB_W:, :])
    m = _sigmoid(g0_ref[...]) * br_pool + _sigmoid(g1_ref[...]) * br_sb + _sigmoid(g2_ref[...]) * br_gdn
    o_ref[...] = x_ref[...] + _dot(m.astype(BF16), wo_ref[...])


def merge(x, o_pool, o_sb, o_g, p, w_branch, w_o, *, tm):
    t, d = x.shape
    gblk = P_GATE // d
    return pl.pallas_call(
        _merge_kernel,
        grid=(t // tm,),
        in_specs=[pl.BlockSpec((tm, d), lambda i: (i, 0)),
                  pl.BlockSpec((tm, POOL_W), lambda i: (i, 0)),
                  pl.BlockSpec((tm, SB_W), lambda i: (i, 0)),
                  pl.BlockSpec((tm, GDN_V_W), lambda i: (i, 0)),
                  pl.BlockSpec((tm, d), lambda i: (i, gblk)),
                  pl.BlockSpec((tm, d), lambda i: (i, gblk + 1)),
                  pl.BlockSpec((tm, d), lambda i: (i, gblk + 2)),
                  pl.BlockSpec((d, d), lambda i: (0, 0)),
                  pl.BlockSpec((d, d), lambda i: (0, 0))],
        out_specs=pl.BlockSpec((tm, d), lambda i: (i, 0)),
        out_shape=jax.ShapeDtypeStruct((t, d), F32),
        compiler_params=_cparams(("parallel",)),
        name="merge",
    )(x, o_pool, o_sb, o_g, p, p, p, w_branch, w_o)


def _pool_select(lane_grp, vals):
    out = vals[-1]
    for gi in range(len(vals) - 2, -1, -1):
        out = jnp.where(lane_grp == gi, vals[gi], out)
    return out


def _pool_prompt_kernel(u_ref, w_ref, sc_ref, o_ref):
    u = u_ref[0]
    t = lax.broadcasted_iota(jnp.int32, u.shape, 0)
    grp = lax.broadcasted_iota(jnp.int32, u.shape, 1) // POOL_GC

    def shifted(x, k):
        return jnp.where(t >= k, pltpu.roll(x, k, axis=0), 0.0)

    sums = []
    s, w = u, 1
    for win in POOL_WINDOWS:
        while w < win:
            s = s + shifted(s, w)
            w *= 2
        sums.append(s)
    sel = _pool_select(grp, sums)
    win = _pool_select(grp, [jnp.full(u.shape, wn, jnp.int32) for wn in POOL_WINDOWS])
    cnt = jnp.minimum(win, t + 1).astype(F32)
    pooled = sel / cnt - u
    o_ref[0] = _dot(pooled.astype(BF16), w_ref[...]) * sc_ref[...]


def pool_prompt(p3, w_bd, scale):
    b, l, _ = p3.shape
    return pl.pallas_call(
        _pool_prompt_kernel,
        grid=(b,),
        in_specs=[pl.BlockSpec((1, l, POOL_W), lambda i: (i, 0, P_POOL // POOL_W)),
                  pl.BlockSpec((POOL_W, POOL_W), lambda i: (0, 0)),
                  pl.BlockSpec((1, POOL_W), lambda i: (0, 0))],
        out_specs=pl.BlockSpec((1, l, POOL_W), lambda i: (i, 0, 0)),
        out_shape=jax.ShapeDtypeStruct((b, l, POOL_W), F32),
        compiler_params=_cparams(("parallel",)),
        name="pool_prompt",
    )(p3, w_bd, scale.reshape(1, POOL_W))


def _sb_weights(z, vis, suffix_mat, c_ref):
    sp = _softplus(z)
    log_1m = -sp if vis is None else jnp.where(vis, -sp, 0.0)
    hi, lo = _split(log_1m)
    s2 = _dot(jnp.concatenate([hi, lo], axis=1), suffix_mat)
    c = c_ref[...]
    a = jnp.exp(z - sp + s2[:, :SB_TK] + c)
    if vis is not None:
        a = jnp.where(vis, a, 0.0)
    c_ref[...] = c + s2[:, SB_TK:]
    return a.astype(BF16)


def _head_pair_rows(x, first_head_lanes):
    return jnp.concatenate([jnp.where(first_head_lanes, x, 0.0), jnp.where(first_head_lanes, 0.0, x)],
                           axis=0).astype(BF16)


def _sb_prompt_kernel(bias_ref, q_ref, k_ref, v_ref, u2_ref, o_ref, acc_ref, c0_ref, c1_ref, *, tq):
    hp = pl.program_id(1)
    qi = pl.program_id(2)
    tk = SB_TK
    biases = (bias_ref[2 * hp], bias_ref[2 * hp + 1])
    q = (q_ref[0] * (SB_HEAD_DIM ** -0.5)).astype(BF16)
    head0 = lax.broadcasted_iota(jnp.int32, (tk, 2 * SB_HEAD_DIM), 1) < SB_HEAD_DIM
    row = qi * tq + lax.broadcasted_iota(jnp.int32, (tq, tk), 0)
    col = lax.broadcasted_iota(jnp.int32, (tq, tk), 1)
    acc_ref[...] = jnp.zeros_like(acc_ref)
    c0_ref[...] = jnp.zeros_like(c0_ref)
    c1_ref[...] = jnp.zeros_like(c1_ref)
    suffix_mat = u2_ref[...]
    nk = (qi + 1) * (tq // tk)

    def body(jj, carry):
        ks = pl.multiple_of((nk - 1 - jj) * tk, tk)
        kk = _head_pair_rows(k_ref[0, pl.ds(ks, tk), :], head0)
        vv = _head_pair_rows(v_ref[0, pl.ds(ks, tk), :], head0)
        z2 = _dot_nt(q, kk)
        vis = (col + ks) < row
        a = [_sb_weights(z2[:, hh * tk:(hh + 1) * tk] + biases[hh], vis, suffix_mat, c_ref)
             for hh, c_ref in enumerate((c0_ref, c1_ref))]
        acc_ref[...] += _dot(jnp.concatenate(a, axis=1), vv)
        return carry

    lax.fori_loop(0, nk, body, 0)
    o_ref[0] = acc_ref[...]


def _suffix_matrix():
    j = np.arange(2 * SB_TK)[:, None] % SB_TK
    c = np.arange(2 * SB_TK)[None, :]
    return jnp.asarray(np.where(c < SB_TK, j > c, True), BF16)


def sb_prompt(p3, bias, *, tq):
    b, l, _ = p3.shape
    hw = 2 * SB_HEAD_DIM
    npair = SB_HEADS // 2
    return pl.pallas_call(
        functools.partial(_sb_prompt_kernel, tq=tq),
        grid=(b, npair, l // tq),
        in_specs=[pl.BlockSpec(memory_space=pltpu.SMEM),
                  pl.BlockSpec((1, tq, hw), lambda i, h, q: (i, q, P_SBQ // hw + h)),
                  pl.BlockSpec((1, l, hw), lambda i, h, q: (i, 0, P_SBK // hw + h)),
                  pl.BlockSpec((1, l, hw), lambda i, h, q: (i, 0, P_SBV // hw + h)),
                  pl.BlockSpec((2 * SB_TK, 2 * SB_TK), lambda i, h, q: (0, 0))],
        out_specs=pl.BlockSpec((1, tq, hw), lambda i, h, q: (i, q, h)),
        out_shape=jax.ShapeDtypeStruct((b, l, SB_W), F32),
        scratch_shapes=[pltpu.VMEM((tq, hw), F32), pltpu.VMEM((tq, SB_TK), F32), pltpu.VMEM((tq, SB_TK), F32)],
        compiler_params=_cparams(("parallel", "parallel", "arbitrary")),
        name="sb_prompt",
    )(bias, p3, p3, p3, _suffix_matrix())


def _sb_decode_kernel(pt_ref, q_ref, bias_ref, k_ref, v_ref, u2_ref, o_ref, qbd_ref, c_ref, acc_ref):
    pg = pl.program_id(1)
    own = (lax.broadcasted_iota(jnp.int32, (SB_HEADS, SB_W), 0)
           == lax.broadcasted_iota(jnp.int32, (SB_HEADS, SB_W), 1) // SB_HEAD_DIM)

    @pl.when(pg == 0)
    def _():
        q = jnp.broadcast_to(q_ref[0] * (SB_HEAD_DIM ** -0.5), (SB_HEADS, SB_W))
        qbd_ref[...] = jnp.where(own, q, 0.0).astype(BF16)
        c_ref[...] = jnp.zeros_like(c_ref)
        acc_ref[...] = jnp.zeros_like(acc_ref)

    z = _dot_nt(qbd_ref[...], k_ref[...].astype(BF16)) + bias_ref[...]
    a = _sb_weights(z, None, u2_ref[...], c_ref)
    acc_ref[...] += _dot(a, v_ref[...].astype(BF16))

    @pl.when(pg == pl.num_programs(1) - 1)
    def _():
        o_ref[0] = jnp.sum(jnp.where(own, acc_ref[...], 0.0), axis=0, keepdims=True)


def sb_decode(q, cache_k, cache_v, page_table, bias, layer):
    bs, n_pages = page_table.shape
    page = cache_k.shape[2]
    assert page == SB_TK

    def page_map(b, pg, pt):
        return (layer, pt[b * n_pages + (n_pages - 1 - pg)], 0, 0)

    grid_spec = pltpu.PrefetchScalarGridSpec(
        num_scalar_prefetch=1,
        grid=(bs, n_pages),
        in_specs=[pl.BlockSpec((1, 1, SB_W), lambda b, pg, pt: (b, 0, 0)),
                  pl.BlockSpec((SB_HEADS, 1), lambda b, pg, pt: (0, 0)),
                  pl.BlockSpec((None, None, page, SB_W), page_map),
                  pl.BlockSpec((None, None, page, SB_W), page_map),
                  pl.BlockSpec((2 * SB_TK, 2 * SB_TK), lambda b, pg, pt: (0, 0))],
        out_specs=pl.BlockSpec((1, 1, SB_W), lambda b, pg, pt: (b, 0, 0)),
        scratch_shapes=[pltpu.VMEM((SB_HEADS, SB_W), BF16), pltpu.VMEM((SB_HEADS, SB_TK), F32),
                        pltpu.VMEM((SB_HEADS, SB_W), F32)],
    )
    out = pl.pallas_call(
        _sb_decode_kernel,
        grid_spec=grid_spec,
        out_shape=jax.ShapeDtypeStruct((bs, 1, SB_W), F32),
        compiler_params=_cparams(("parallel", "arbitrary")),
        name="sb_decode",
    )(page_table.reshape(-1), q.reshape(bs, 1, SB_W), bias.reshape(SB_HEADS, 1), cache_k, cache_v,
      _suffix_matrix())
    return out.reshape(bs, SB_W)


def _head_indicator(width, head):
    i = np.arange(width) // head
    return jnp.asarray(i[:, None] == i[None, :], BF16)


def _gdn_qkv_post(conv_out, seg):
    act = conv_out * _sigmoid(conv_out)
    q = act[:, :GDN_QK_W]
    k = act[:, GDN_QK_W:2 * GDN_QK_W]
    v = act[:, 2 * GDN_QK_W:]
    qn = q * lax.rsqrt(_dot_exact_rhs(q * q, seg) + L2_EPS) * (GDN_DK ** -0.5)
    kn = k * lax.rsqrt(_dot_exact_rhs(k * k, seg) + L2_EPS)
    return qn, kn, v


def _gdn_gates(ab, alog, dtb):
    g = -jnp.exp(alog) * _softplus(ab + dtb)
    return g, _sigmoid(ab)


def _gdn_pre_kernel(x_ref, halo_ref, ab_ref, cw_ref, alog_ref, dtb_ref, seg_ref, q_ref, k_ref, v_ref, gb_ref):
    i = pl.program_id(1)
    x = x_ref[0]
    tl = x.shape[0]
    halo = jnp.where(i > 0, halo_ref[0], 0.0)
    nh = halo.shape[0]
    ext = jnp.concatenate([halo, x], axis=0)
    out = None
    for tap in range(GDN_CONV):
        d = GDN_CONV - 1 - tap
        src = x if d == 0 else pltpu.roll(ext, d, axis=0)[nh:]
        term = src * cw_ref[tap:tap + 1, :]
        out = term if out is None else out + term
    qn, kn, v = _gdn_qkv_post(out, seg_ref[...])
    q_ref[0] = qn
    k_ref[0] = kn
    v_ref[0] = v

    g, beta = _gdn_gates(ab_ref[0], alog_ref[...], dtb_ref[...])
    tt = lax.broadcasted_iota(jnp.int32, g.shape, 0) % GDN_CHUNK
    k = 1
    while k < GDN_CHUNK:
        g = g + jnp.where(tt >= k, pltpu.roll(g, k, axis=0), 0.0)
        k *= 2
    lane = lax.broadcasted_iota(jnp.int32, g.shape, 1)
    gb_ref[0] = jnp.where(lane < GDN_HEADS, g, beta)


def gdn_pre(p3, conv_w, alog, dtb, *, tl):
    b, l, _ = p3.shape
    nh = 8
    out_sd = jax.ShapeDtypeStruct((b, l, GDN_QK_W), F32)
    return pl.pallas_call(
        _gdn_pre_kernel,
        grid=(b, l // tl),
        in_specs=[pl.BlockSpec((1, tl, GDN_QKV_W), lambda n, i: (n, i, P_GQKV // GDN_QKV_W)),
                  pl.BlockSpec((1, nh, GDN_QKV_W), lambda n, i: (n, jnp.maximum(i * (tl // nh) - 1, 0), 0)),
                  pl.BlockSpec((1, tl, LANES), lambda n, i: (n, i, P_AB // LANES)),
                  pl.BlockSpec((GDN_CONV, GDN_QKV_W), lambda n, i: (0, 0)),
                  pl.BlockSpec((1, LANES), lambda n, i: (0, 0)),
                  pl.BlockSpec((1, LANES), lambda n, i: (0, 0)),
                  pl.BlockSpec((GDN_QK_W, GDN_QK_W), lambda n, i: (0, 0))],
        out_specs=[pl.BlockSpec((1, tl, GDN_QK_W), lambda n, i: (n, i, 0))] * 3
        + [pl.BlockSpec((1, tl, LANES), lambda n, i: (n, i, 0))],
        out_shape=[out_sd, out_sd, out_sd, jax.ShapeDtypeStruct((b, l, LANES), F32)],
        compiler_params=_cparams(("parallel", "parallel")),
        name="gdn_pre",
    )(p3, p3, p3, conv_w, alog, dtb, _head_indicator(GDN_QK_W, GDN_DK))


def _gate_broadcast_matrix():
    r = np.arange(2 * LANES)[:, None] % LANES
    c = np.arange(2 * LANES)[None, :]
    return jnp.asarray(np.where(c < LANES, r < GDN_HEADS, (r >= GDN_HEADS) & (r < 2 * GDN_HEADS)), BF16)


def _gdn_chunk_kernel(q_ref, k_ref, v_ref, gb_ref, z_ref, gn_ref, e_ref, seg_ref, og_ref, s_ref, st_ref):
    c = pl.program_id(1)
    ch = GDN_CHUNK
    r = GDN_HEADS * ch
    sh = ch.bit_length() - 1

    @pl.when(c == 0)
    def _():
        st_ref[...] = jnp.zeros_like(st_ref)

    ri = lax.broadcasted_iota(jnp.int32, (r, r), 0)
    ci = lax.broadcasted_iota(jnp.int32, (r, r), 1)
    same_head = (ri >> sh) == (ci >> sh)

    def stack(x):
        return jnp.where(same_head, jnp.concatenate([x] * GDN_HEADS, axis=0), 0.0)

    kbd, qbd, vbd = stack(k_ref[0]), stack(q_ref[0]), stack(v_ref[0])

    gbt = jnp.concatenate([gb_ref[0]] * GDN_HEADS, axis=0)
    l1 = lax.broadcasted_iota(jnp.int32, gbt.shape, 1)
    r1 = lax.broadcasted_iota(jnp.int32, gbt.shape, 0) >> sh
    hi, lo = _split(jnp.where((l1 == r1) | (l1 == r1 + GDN_HEADS), gbt, 0.0))
    gbc = _dot(jnp.concatenate([hi, lo], axis=1), e_ref[...])
    gcol = jnp.concatenate([gbc[:, :LANES]] * (r // LANES), axis=1)
    bcol = jnp.concatenate([gbc[:, LANES:]] * (r // LANES), axis=1)
    glcol = jnp.concatenate([jnp.broadcast_to(gcol[(h + 1) * ch - 1:(h + 1) * ch, :], (ch, r))
                             for h in range(GDN_HEADS)], axis=0)
    grow = gcol.T

    incl = same_head & (ri >= ci)
    strict = same_head & (ri > ci)
    decay = jnp.where(incl, jnp.exp(jnp.where(incl, gcol - grow, 0.0)), 0.0)
    kb = kbd * bcol
    a_mat = jnp.where(strict, _dot_nt(kb.astype(BF16), kbd.astype(BF16)) * decay, 0.0)

    t_inv = (ri == ci).astype(F32) - jnp.where((ri >> 1) == (ci >> 1), a_mat, 0.0)
    for lv in range(1, sh):
        off = ((ri >> (lv + 1)) == (ci >> (lv + 1))) & ((ri >> lv) != (ci >> lv))
        t_inv = t_inv - _mm3(t_inv, _mm3(jnp.where(off, a_mat, 0.0), t_inv))

    eg = jnp.exp(gcol)
    u = _mm3(t_inv, vbd * bcol)
    w = _mm3(t_inv, kb * eg)
    qk = jnp.where(incl, _dot_nt(qbd.astype(BF16), kbd.astype(BF16)) * decay, 0.0)

    s = st_ref[...]
    sb = s.astype(BF16)
    v_new = u - _dot(w.astype(BF16), sb)
    vnb = v_new.astype(BF16)
    o = _dot((qbd * eg).astype(BF16), sb) + _dot(qk.astype(BF16), vnb)
    kd = kbd * jnp.exp(glcol - gcol)
    s_new = s * jnp.exp(glcol) + _dot_tn(kd.astype(BF16), vnb)
    st_ref[...] = s_new

    oc = o[0:ch]
    for h in range(1, GDN_HEADS):
        oc = oc + o[h * ch:(h + 1) * ch]
    ms = _dot_exact_rhs(oc * oc, seg_ref[...]) * (1.0 / GDN_DV)
    zz = z_ref[0]
    og_ref[0] = oc * lax.rsqrt(ms + RMS_EPS) * gn_ref[...] * (zz * _sigmoid(zz))

    @pl.when(c == pl.num_programs(1) - 1)
    def _():
        for h in range(GDN_HEADS):
            s_ref[0, h] = s_new[h * GDN_DK:(h + 1) * GDN_DK, h * GDN_DV:(h + 1) * GDN_DV]


def gdn_chunk(qn, kn, v, gb, p3, gnorm):
    b, l, _ = qn.shape
    ch = GDN_CHUNK
    r = GDN_HEADS * ch
    blk = lambda w: pl.BlockSpec((1, ch, w), lambda n, c: (n, c, 0))
    return pl.pallas_call(
        _gdn_chunk_kernel,
        grid=(b, l // ch),
        in_specs=[blk(GDN_QK_W), blk(GDN_QK_W), blk(GDN_V_W), blk(LANES),
                  pl.BlockSpec((1, ch, GDN_V_W), lambda n, c: (n, c, P_GZ // GDN_V_W)),
                  pl.BlockSpec((1, GDN_V_W), lambda n, c: (0, 0)),
                  pl.BlockSpec((2 * LANES, 2 * LANES), lambda n, c: (0, 0)),
                  pl.BlockSpec((GDN_V_W, GDN_V_W), lambda n, c: (0, 0))],
        out_specs=[pl.BlockSpec((1, ch, GDN_V_W), lambda n, c: (n, c, 0)),
                   pl.BlockSpec((1, GDN_HEADS, GDN_DK, GDN_DV), lambda n, c: (n, 0, 0, 0))],
        out_shape=[jax.ShapeDtypeStruct((b, l, GDN_V_W), F32),
                   jax.ShapeDtypeStruct((b, GDN_HEADS, GDN_DK, GDN_DV), F32)],
        scratch_shapes=[pltpu.VMEM((r, r), F32)],
        compiler_params=_cparams(("parallel", "arbitrary")),
        name="gdn_chunk",
    )(qn, kn, v, gb, p3, jnp.tile(gnorm, GDN_HEADS).reshape(1, GDN_V_W), _gate_broadcast_matrix(),
      _head_indicator(GDN_V_W, GDN_DV))


def _sample_pre_kernel(up_ref, x_ref, ab_ref, ph_ref, ch_ref, pw_ref, psc_ref, cw_ref, alog_ref, dtb_ref, seg_ref,
                       op_ref, q_ref, k_ref, v_ref, gb_ref, *, pos0):
    u = up_ref[...]
    ph = ph_ref[...]
    hist = ph.shape[1]
    ti = lax.broadcasted_iota(jnp.int32, ph.shape, 1)
    grp = lax.broadcasted_iota(jnp.int32, u.shape, 1) // POOL_GC
    means = []
    for win in POOL_WINDOWS:
        n_hist = min(win - 1, hist)
        tail = jnp.sum(jnp.where(ti >= hist - n_hist, ph, 0.0), axis=1)
        means.append((tail + u) / float(min(win, pos0 + 1)))
    pooled = _pool_select(grp, means) - u
    op_ref[...] = _dot(pooled.astype(BF16), pw_ref[...]) * psc_ref[...]

    cw = cw_ref[...]
    out = jnp.sum(ch_ref[...] * cw[None, :GDN_CONV - 1, :], axis=1) + x_ref[...] * cw[GDN_CONV - 1:, :]
    qn, kn, v = _gdn_qkv_post(out, seg_ref[...])
    q_ref[...] = qn
    k_ref[...] = kn
    v_ref[...] = v

    g, beta = _gdn_gates(ab_ref[...], alog_ref[...], dtb_ref[...])
    lane = lax.broadcasted_iota(jnp.int32, g.shape, 1)
    gb_ref[...] = jnp.where(lane < GDN_HEADS, jnp.exp(g), beta)


def sample_pre(p, pool_hist, conv_hist, pool_w_bd, pool_scale, conv_w, alog, dtb, *, pos0):
    bs = p.shape[0]
    full = lambda shape: pl.BlockSpec(shape, lambda i: (0,) * len(shape))
    sd = lambda w: jax.ShapeDtypeStruct((bs, w), F32)
    return pl.pallas_call(
        functools.partial(_sample_pre_kernel, pos0=pos0),
        grid=(1,),
        in_specs=[pl.BlockSpec((bs, POOL_W), lambda i: (0, P_POOL // POOL_W)),
                  pl.BlockSpec((bs, GDN_QKV_W), lambda i: (0, P_GQKV // GDN_QKV_W)),
                  pl.BlockSpec((bs, LANES), lambda i: (0, P_AB // LANES)),
                  full(pool_hist.shape), full(conv_hist.shape), full((POOL_W, POOL_W)), full((1, POOL_W)),
                  full((GDN_CONV, GDN_QKV_W)), full((1, LANES)), full((1, LANES)), full((GDN_QK_W, GDN_QK_W))],
        out_specs=[full((bs, POOL_W)), full((bs, GDN_QK_W)), full((bs, GDN_QK_W)), full((bs, GDN_V_W)),
                   full((bs, LANES))],
        out_shape=[sd(POOL_W), sd(GDN_QK_W), sd(GDN_QK_W), sd(GDN_V_W), sd(LANES)],
        compiler_params=_cparams(("arbitrary",)),
        name="sample_pre",
    )(p, p, p, pool_hist, conv_hist, pool_w_bd, pool_scale.reshape(1, POOL_W), conv_w, alog, dtb,
      _head_indicator(GDN_QK_W, GDN_DK))


def _gdn_step_kernel(s_ref, k_ref, q_ref, v_ref, z_ref, eg_ref, beta_ref, gn_ref, so_ref, og_ref):
    k = k_ref[...]
    s = s_ref[...] * eg_ref[...]
    delta = (v_ref[...] - jnp.sum(k * s, axis=2, keepdims=True)) * beta_ref[...]
    s = s + k * delta
    so_ref[...] = s
    o = jnp.sum(q_ref[...] * s, axis=2, keepdims=True)
    zz = z_ref[...]
    og_ref[...] = o * _rms_scale(o) * gn_ref[...] * (zz * _sigmoid(zz))


def gdn_step(state, kn, qn, v, z, eg, beta, gnorm, *, bt):
    bs = state.shape[0]
    h, dk, dv = GDN_HEADS, GDN_DK, GDN_DV
    blk = lambda a, c: pl.BlockSpec((bt, h, a, c), lambda i: (i, 0, 0, 0))
    return pl.pallas_call(
        _gdn_step_kernel,
        grid=(bs // bt,),
        in_specs=[blk(dk, dv), blk(dk, 1), blk(dk, 1), blk(1, dv), blk(1, dv), blk(1, 1), blk(1, 1),
                  pl.BlockSpec((1, 1, 1, dv), lambda i: (0, 0, 0, 0))],
        out_specs=[blk(dk, dv), blk(1, dv)],
        out_shape=[jax.ShapeDtypeStruct((bs, h, dk, dv), F32), jax.ShapeDtypeStruct((bs, h, 1, dv), F32)],
        compiler_params=_cparams(("parallel",)),
        name="gdn_step",
    )(state, kn.reshape(bs, h, dk, 1), qn.reshape(bs, h, dk, 1), v.reshape(bs, h, 1, dv), z.reshape(bs, h, 1, dv),
      eg.reshape(bs, h, 1, 1), beta.reshape(bs, h, 1, 1), gnorm.reshape(1, 1, 1, dv))


def _largest_tile(n, cap):
    t = min(n, cap)
    while n % t:
        t //= 2
    return t


def _pack_w_in(w_in):
    depth, d, _ = w_in.shape
    pad = jnp.zeros((depth, d, P_GATE - P_AB - (R_GZ - R_GA)), w_in.dtype)
    return jnp.concatenate([w_in[..., R_GQKV:R_GA], w_in[..., R_POOL:R_GQKV], w_in[..., R_GZ:R_GATE],
                            w_in[..., R_GA:R_GZ], pad, w_in[..., R_GATE:R_END]], axis=-1).astype(BF16)


def _lane_pad(x):
    return jnp.pad(x, ((0, 0), (0, LANES - x.shape[1])))


def kernel(x_prompt, x_sample, cache_sb_k, cache_sb_v, page_table, state_pool, state_gdn_conv, state_gdn, norm1_g, w_in, pool_w, pool_scale, sb_bias, gdn_conv_w, gdn_a_log, gdn_dt_bias, gdn_norm_g, w_branch, w_o, norm2_g, w_ffn_in, w_ffn_out, normf_g):
    bp, seq, d = x_prompt.shape
    bs = x_sample.shape[0]
    depth = w_in.shape[0]
    n_phys, page = cache_sb_k.shape[1], cache_sb_k.shape[2]
    past_len = page_table.shape[1] * page
    tp = bp * seq

    w_in_p = _pack_w_in(w_in)
    w_branch_b = w_branch.astype(BF16)
    w_o_b = w_o.astype(BF16)
    w_ffn_in_b = w_ffn_in.astype(BF16)
    w_ffn_out_b = w_ffn_out.astype(BF16)
    eye_g = jnp.eye(len(POOL_WINDOWS), dtype=F32)
    pool_w_bd = jnp.einsum("lgcd,gh->lgchd", pool_w, eye_g).reshape(depth, POOL_W, POOL_W).astype(BF16)
    alog_p = _lane_pad(gdn_a_log)
    dtb_p = _lane_pad(gdn_dt_bias)
    cache_k = cache_sb_k.reshape(depth, n_phys, page, SB_W)
    cache_v = cache_sb_v.reshape(depth, n_phys, page, SB_W)

    tm_p = _largest_tile(tp, 1024)
    tm_merge = _largest_tile(tp, 512)
    tm_s = _largest_tile(bs, 128)
    tq = _largest_tile(seq, 256)
    tl = _largest_tile(seq, 256)
    bt = _largest_tile(bs, 8)

    xp = x_prompt.reshape(tp, d)
    xs = x_sample.reshape(bs, d)
    outs = [[] for _ in range(10)]
    for l in range(depth):
        final = l == depth - 1
        p = rms_matmul(xp, norm1_g[l], w_in_p[l], tm=tm_p, tn=1024)
        p3 = p.reshape(bp, seq, P_W)
        o_pool = pool_prompt(p3, pool_w_bd[l], pool_scale[l])
        o_sb = sb_prompt(p3, sb_bias[l], tq=tq)
        qn, kn, gv, gb = gdn_pre(p3, gdn_conv_w[l], alog_p[l:l + 1], dtb_p[l:l + 1], tl=tl)
        o_g, s_p = gdn_chunk(qn, kn, gv, gb, p3, gdn_norm_g[l])
        xp = merge(xp, o_pool.reshape(tp, POOL_W), o_sb.reshape(tp, SB_W), o_g.reshape(tp, GDN_V_W), p,
                   w_branch_b[l], w_o_b[l], tm=tm_merge)
        xp = ffn(xp, norm2_g[l], w_ffn_in_b[l], w_ffn_out_b[l], normf_g, tm=tm_p, tf=256, final=final)
        outs[0].append(p3[..., P_SBK:P_SBV].reshape(bp, seq, SB_HEADS, SB_HEAD_DIM))
        outs[1].append(p3[..., P_SBV:P_GZ].reshape(bp, seq, SB_HEADS, SB_HEAD_DIM))
        outs[2].append(p3[:, seq - POOL_HIST:, P_POOL:P_SBQ])
        outs[3].append(p3[:, seq - (GDN_CONV - 1):, P_GQKV:P_POOL])
        outs[4].append(s_p)

        ps = rms_matmul(xs, norm1_g[l], w_in_p[l], tm=tm_s, tn=1024)
        o_pool_s, qn_s, kn_s, gv_s, gb_s = sample_pre(ps, state_pool[l], state_gdn_conv[l], pool_w_bd[l],
                                                      pool_scale[l], gdn_conv_w[l], alog_p[l:l + 1],
                                                      dtb_p[l:l + 1], pos0=past_len)
        o_sb_s = sb_decode(ps[:, P_SBQ:P_SBK], cache_k, cache_v, page_table, sb_bias[l], l)
        s_s, o_g_s = gdn_step(state_gdn[l], kn_s, qn_s, gv_s, ps[:, P_GZ:P_AB], gb_s[:, :GDN_HEADS],
                              gb_s[:, GDN_HEADS:2 * GDN_HEADS], gdn_norm_g[l], bt=bt)
        xs = merge(xs, o_pool_s, o_sb_s, o_g_s.reshape(bs, GDN_V_W), ps, w_branch_b[l], w_o_b[l], tm=tm_s)
        xs = ffn(xs, norm2_g[l], w_ffn_in_b[l], w_ffn_out_b[l], normf_g, tm=tm_s, tf=256, final=final)
        outs[5].append(ps[:, P_SBK:P_SBV].reshape(bs, 1, SB_HEADS, SB_HEAD_DIM))
        outs[6].append(ps[:, P_SBV:P_GZ].reshape(bs, 1, SB_HEADS, SB_HEAD_DIM))
        outs[7].append(jnp.concatenate([state_pool[l], ps[:, None, P_POOL:P_SBQ]], axis=1)[:, 1:])
        outs[8].append(jnp.concatenate([state_gdn_conv[l], ps[:, None, P_GQKV:P_POOL]], axis=1)[:, 1:])
        outs[9].append(s_s)

    y_prompt = xp.reshape(bp, seq, d)
    y_sample = xs.reshape(bs, 1, d)
    return (y_prompt, y_sample) + tuple(jnp.stack(o) for o in outs)
```

```python
import functools

import numpy as np
import jax
import jax.numpy as jnp
from jax import lax
from jax.experimental import pallas as pl
from jax.experimental.pallas import tpu as pltpu

F32 = jnp.float32
BF16 = jnp.bfloat16

D_MODEL = 1024
POOL_WINDOWS = (2, 4, 8, 16)
POOL_W = 256
POOL_GC = 64
POOL_HIST = 15
SB_HEADS = 8
SB_HEAD_DIM = 64
SB_W = 512
GDN_HEADS = 4
GDN_DK = 64
GDN_DV = 64
GDN_QK_W = 256
GDN_V_W = 256
GDN_QKV_W = 768
GDN_CONV = 4
GDN_CHUNK = 64
D_FF = 2816
RMS_EPS = 1e-6
L2_EPS = 1e-6

R_POOL, R_SBQ, R_GQKV, R_GA, R_GZ, R_GATE, R_END = 0, 256, 1792, 2560, 2568, 2824, 5896
P_GQKV, P_POOL, P_SBQ, P_SBK, P_SBV, P_GZ, P_AB, P_GATE, P_W = 0, 768, 1024, 1536, 2048, 2560, 2816, 3072, 6144

LANES = 128
SB_TK = LANES
VMEM_LIMIT = 56 * 1024 * 1024


def _cparams(sem):
    return pltpu.CompilerParams(dimension_semantics=sem, vmem_limit_bytes=VMEM_LIMIT)


def _sigmoid(x):
    return 1.0 / (1.0 + jnp.exp(-x))


def _softplus(x):
    return jnp.maximum(x, 0.0) + jnp.log(1.0 + jnp.exp(-jnp.abs(x)))


def _split(x):
    hi = x.astype(BF16)
    lo = (x - hi.astype(F32)).astype(BF16)
    return hi, lo


def _dot(a, b):
    return jnp.dot(a, b, preferred_element_type=F32)


def _dot_nt(a, b):
    return lax.dot_general(a, b, (((1,), (1,)), ((), ())), preferred_element_type=F32)


def _dot_tn(a, b):
    return lax.dot_general(a, b, (((0,), (0,)), ((), ())), preferred_element_type=F32)


def _mm3(a, b):
    ah, al = _split(a)
    bh, bl = _split(b)
    return _dot(ah, bh) + (_dot(ah, bl) + _dot(al, bh))


def _dot_exact_rhs(x, m):
    hi, lo = _split(x)
    return _dot(hi, m) + _dot(lo, m)


def _rms_scale(x):
    return lax.rsqrt(jnp.mean(x * x, axis=-1, keepdims=True) + RMS_EPS)


def _rms_matmul_kernel(x_ref, g_ref, w_ref, o_ref, h_ref):
    @pl.when(pl.program_id(1) == 0)
    def _():
        x = x_ref[...]
        h_ref[...] = (x * _rms_scale(x) * g_ref[...]).astype(BF16)

    o_ref[...] = _dot(h_ref[...], w_ref[...])


def rms_matmul(x, g, w, *, tm, tn):
    t, d = x.shape
    n = w.shape[1]
    return pl.pallas_call(
        _rms_matmul_kernel,
        grid=(t // tm, n // tn),
        in_specs=[pl.BlockSpec((tm, d), lambda i, j: (i, 0)),
                  pl.BlockSpec((1, d), lambda i, j: (0, 0)),
                  pl.BlockSpec((d, tn), lambda i, j: (0, j))],
        out_specs=pl.BlockSpec((tm, tn), lambda i, j: (i, j)),
        out_shape=jax.ShapeDtypeStruct((t, n), F32),
        scratch_shapes=[pltpu.VMEM((tm, d), BF16)],
        compiler_params=_cparams(("parallel", "arbitrary")),
        name="in_proj",
    )(x, g.reshape(1, d), w)


def _ffn_kernel(x_ref, g_ref, wg_ref, wu_ref, wo_ref, gf_ref, o_ref, h_ref, acc_ref, *, final):
    f = pl.program_id(1)

    @pl.when(f == 0)
    def _():
        x = x_ref[...]
        h_ref[...] = (x * _rms_scale(x) * g_ref[...]).astype(BF16)
        acc_ref[...] = x

    h = h_ref[...]
    gate = _dot(h, wg_ref[...])
    up = _dot(h, wu_ref[...])
    act = (gate * _sigmoid(gate)) * up
    acc_ref[...] += _dot(act.astype(BF16), wo_ref[...])

    @pl.when(f == pl.num_programs(1) - 1)
    def _():
        y = acc_ref[...]
        if final:
            y = y * _rms_scale(y) * gf_ref[...]
        o_ref[...] = y


def ffn(x, g, w_in, w_out, gf, *, tm, tf, final):
    t, d = x.shape
    dff = w_out.shape[0]
    nf = dff // tf
    return pl.pallas_call(
        functools.partial(_ffn_kernel, final=final),
        grid=(t // tm, nf),
        in_specs=[pl.BlockSpec((tm, d), lambda i, f: (i, 0)),
                  pl.BlockSpec((1, d), lambda i, f: (0, 0)),
                  pl.BlockSpec((d, tf), lambda i, f: (0, f)),
                  pl.BlockSpec((d, tf), lambda i, f: (0, f + nf)),
                  pl.BlockSpec((tf, d), lambda i, f: (f, 0)),
                  pl.BlockSpec((1, d), lambda i, f: (0, 0))],
        out_specs=pl.BlockSpec((tm, d), lambda i, f: (i, 0)),
        out_shape=jax.ShapeDtypeStruct((t, d), F32),
        scratch_shapes=[pltpu.VMEM((tm, d), BF16), pltpu.VMEM((tm, d), F32)],
        compiler_params=_cparams(("parallel", "arbitrary")),
        name="ffn",
    )(x, g.reshape(1, d), w_in, w_in, w_out, gf.reshape(1, d))


def _merge_kernel(x_ref, op_ref, osb_ref, og_ref, g0_ref, g1_ref, g2_ref, wb_ref, wo_ref, o_ref):
    br_pool = _dot(op_ref[...].astype(BF16), wb_ref[0:POOL_W, :])
    br_sb = _dot(osb_ref[...].astype(BF16), wb_ref[POOL_W:POOL_W + SB_W, :])
    br_gdn = _dot(og_ref[...].astype(BF16), wb_ref[POOL_W + SB_W:, :])
    m = _sigmoid(g0_ref[...]) * br_pool + _sigmoid(g1_ref[...]) * br_sb + _sigmoid(g2_ref[...]) * br_gdn
    o_ref[...] = x_ref[...] + _dot(m.astype(BF16), wo_ref[...])


def merge(x, o_pool, o_sb, o_g, p, w_branch, w_o, *, tm):
    t, d = x.shape
    gblk = P_GATE // d
    return pl.pallas_call(
        _merge_kernel,
        grid=(t // tm,),
        in_specs=[pl.BlockSpec((tm, d), lambda i: (i, 0)),
                  pl.BlockSpec((tm, POOL_W), lambda i: (i, 0)),
                  pl.BlockSpec((tm, SB_W), lambda i: (i, 0)),
                  pl.BlockSpec((tm, GDN_V_W), lambda i: (i, 0)),
                  pl.BlockSpec((tm, d), lambda i: (i, gblk)),
                  pl.BlockSpec((tm, d), lambda i: (i, gblk + 1)),
                  pl.BlockSpec((tm, d), lambda i: (i, gblk + 2)),
                  pl.BlockSpec((d, d), lambda i: (0, 0)),
                  pl.BlockSpec((d, d), lambda i: (0, 0))],
        out_specs=pl.BlockSpec((tm, d), lambda i: (i, 0)),
        out_shape=jax.ShapeDtypeStruct((t, d), F32),
        compiler_params=_cparams(("parallel",)),
        name="merge",
    )(x, o_pool, o_sb, o_g, p, p, p, w_branch, w_o)


def _pool_select(lane_grp, vals):
    out = vals[-1]
    for gi in range(len(vals) - 2, -1, -1):
        out = jnp.where(lane_grp == gi, vals[gi], out)
    return out


def _pool_prompt_kernel(u_ref, w_ref, sc_ref, o_ref):
    u = u_ref[0]
    t = lax.broadcasted_iota(jnp.int32, u.shape, 0)
    grp = lax.broadcasted_iota(jnp.int32, u.shape, 1) // POOL_GC

    def shifted(x, k):
        return jnp.where(t >= k, pltpu.roll(x, k, axis=0), 0.0)

    sums = []
    s, w = u, 1
    for win in POOL_WINDOWS:
        while w < win:
            s = s + shifted(s, w)
            w *= 2
        sums.append(s)
    sel = _pool_select(grp, sums)
    win = _pool_select(grp, [jnp.full(u.shape, wn, jnp.int32) for wn in POOL_WINDOWS])
    cnt = jnp.minimum(win, t + 1).astype(F32)
    pooled = sel / cnt - u
    o_ref[0] = _dot(pooled.astype(BF16), w_ref[...]) * sc_ref[...]


def pool_prompt(p3, w_bd, scale):
    b, l, _ = p3.shape
    return pl.pallas_call(
        _pool_prompt_kernel,
        grid=(b,),
        in_specs=[pl.BlockSpec((1, l, POOL_W), lambda i: (i, 0, P_POOL // POOL_W)),
                  pl.BlockSpec((POOL_W, POOL_W), lambda i: (0, 0)),
                  pl.BlockSpec((1, POOL_W), lambda i: (0, 0))],
        out_specs=pl.BlockSpec((1, l, POOL_W), lambda i: (i, 0, 0)),
        out_shape=jax.ShapeDtypeStruct((b, l, POOL_W), F32),
        compiler_params=_cparams(("parallel",)),
        name="pool_prompt",
    )(p3, w_bd, scale.reshape(1, POOL_W))


def _sb_log_terms(z, vis, suffix_mat):
    sp = _softplus(z)
    log_1m = -sp if vis is None else jnp.where(vis, -sp, 0.0)
    hi, lo = _split(log_1m)
    return sp, _dot(jnp.concatenate([hi, lo], axis=1), suffix_mat)


def _head_pair_rows(x, first_head_lanes):
    return jnp.concatenate([jnp.where(first_head_lanes, x, 0.0), jnp.where(first_head_lanes, 0.0, x)],
                           axis=0).astype(BF16)


def _sb_prompt_kernel(bias_ref, q_ref, k_ref, v_ref, u2_ref, o_ref, qb_ref, acc_ref, c_ref, *, tq):
    qi = pl.program_id(1)
    tk = SB_TK
    hw = 2 * SB_HEAD_DIM
    npair = SB_HEADS // 2
    n_diag = tq // tk
    head0 = lax.broadcasted_iota(jnp.int32, (tk, hw), 1) < SB_HEAD_DIM
    row = lax.broadcasted_iota(jnp.int32, (tq, tk), 0)
    col = lax.broadcasted_iota(jnp.int32, (tq, tk), 1)
    qb_ref[...] = (q_ref[0] * (SB_HEAD_DIM ** -0.5)).astype(BF16)
    acc_ref[...] = jnp.zeros_like(acc_ref)
    c_ref[...] = jnp.zeros_like(c_ref)
    suffix_mat = u2_ref[...]

    def key_block(ks, vis):
        pairs = [slice(p * hw, (p + 1) * hw) for p in range(npair)]
        z2 = [_dot_nt(qb_ref[:, lanes], _head_pair_rows(k_ref[0, pl.ds(ks, tk), lanes], head0)) for lanes in pairs]
        z = [z2[h // 2][:, (h % 2) * tk:(h % 2 + 1) * tk] + bias_ref[h] for h in range(SB_HEADS)]
        terms = [_sb_log_terms(zh, vis, suffix_mat) for zh in z]
        a = []
        for h, (zh, (sp, s2)) in enumerate(zip(z, terms)):
            c = c_ref[h]
            ah = jnp.exp(zh - sp + s2[:, :tk] + c)
            a.append((ah if vis is None else jnp.where(vis, ah, 0.0)).astype(BF16))
            c_ref[h] = c + s2[:, tk:]
        for p, lanes in enumerate(pairs):
            vv = _head_pair_rows(v_ref[0, pl.ds(ks, tk), lanes], head0)
            acc_ref[:, lanes] += _dot(jnp.concatenate(a[2 * p:2 * p + 2], axis=1), vv)

    for d in range(n_diag - 1, -1, -1):
        key_block(pl.multiple_of(qi * tq + d * tk, tk), (col + d * tk) < row)

    def full_body(jj, carry):
        key_block(pl.multiple_of((qi * n_diag - 1 - jj) * tk, tk), None)
        return carry

    lax.fori_loop(0, qi * n_diag, full_body, 0)
    o_ref[0] = acc_ref[...]


def _suffix_matrix():
    j = np.arange(2 * SB_TK)[:, None] % SB_TK
    c = np.arange(2 * SB_TK)[None, :]
    return jnp.asarray(np.where(c < SB_TK, j > c, True), BF16)


def sb_prompt(p3, bias, *, tq):
    b, l, _ = p3.shape
    return pl.pallas_call(
        functools.partial(_sb_prompt_kernel, tq=tq),
        grid=(b, l // tq),
        in_specs=[pl.BlockSpec(memory_space=pltpu.SMEM),
                  pl.BlockSpec((1, tq, SB_W), lambda i, q: (i, q, P_SBQ // SB_W)),
                  pl.BlockSpec((1, l, SB_W), lambda i, q: (i, 0, P_SBK // SB_W)),
                  pl.BlockSpec((1, l, SB_W), lambda i, q: (i, 0, P_SBV // SB_W)),
                  pl.BlockSpec((2 * SB_TK, 2 * SB_TK), lambda i, q: (0, 0))],
        out_specs=pl.BlockSpec((1, tq, SB_W), lambda i, q: (i, q, 0)),
        out_shape=jax.ShapeDtypeStruct((b, l, SB_W), F32),
        scratch_shapes=[pltpu.VMEM((tq, SB_W), BF16), pltpu.VMEM((tq, SB_W), F32),
                        pltpu.VMEM((SB_HEADS, tq, SB_TK), F32)],
        compiler_params=_cparams(("parallel", "arbitrary")),
        name="sb_prompt",
    )(bias, p3, p3, p3, _suffix_matrix())


def _sb_decode_kernel(pt_ref, q_ref, bias_ref, u2_ref, *refs, group):
    k_refs, v_refs = refs[:group], refs[group:2 * group]
    o_ref, c_ref, acc_ref = refs[2 * group:]
    step = pl.program_id(1)
    nh = SB_HEADS
    own = (lax.broadcasted_iota(jnp.int32, (nh, SB_W), 0)
           == lax.broadcasted_iota(jnp.int32, (nh, SB_W), 1) // SB_HEAD_DIM)

    @pl.when(step == 0)
    def _():
        c_ref[...] = jnp.zeros_like(c_ref)
        acc_ref[...] = jnp.zeros_like(acc_ref)

    q = jnp.broadcast_to(q_ref[0] * (SB_HEAD_DIM ** -0.5), (nh, SB_W))
    qbd = jnp.where(own, q, 0.0).astype(BF16)
    z = jnp.concatenate([_dot(qbd, k_refs[g][...].astype(BF16)) + bias_ref[...] for g in range(group)], axis=0)
    sp, s2 = _sb_log_terms(z, None, u2_ref[...])
    later = [None] * group
    run = c_ref[...]
    for g in range(group - 1, -1, -1):
        later[g] = run
        run = run + s2[g * nh:(g + 1) * nh, SB_TK:]
    c_ref[...] = run
    a = jnp.exp(z - sp + s2[:, :SB_TK] + jnp.concatenate(later, axis=0))
    acc = acc_ref[...]
    for g in range(group):
        acc = acc + _dot_nt(a[g * nh:(g + 1) * nh].astype(BF16), v_refs[g][...].astype(BF16))
    acc_ref[...] = acc

    @pl.when(step == pl.num_programs(1) - 1)
    def _():
        o_ref[0] = jnp.sum(jnp.where(own, acc, 0.0), axis=0, keepdims=True)


def sb_decode(q, cache_kt, cache_vt, page_table, bias, layer, *, group):
    bs, n_pages = page_table.shape
    page = cache_kt.shape[3]
    assert page == SB_TK and n_pages % group == 0
    n_steps = n_pages // group

    def page_spec(g):
        def index(b, s, pt):
            return (layer, pt[b * n_pages + (n_steps - 1 - s) * group + g], 0, 0)
        return pl.BlockSpec((None, None, SB_W, page), index)

    grid_spec = pltpu.PrefetchScalarGridSpec(
        num_scalar_prefetch=1,
        grid=(bs, n_steps),
        in_specs=[pl.BlockSpec((1, 1, SB_W), lambda b, s, pt: (b, 0, 0)),
                  pl.BlockSpec((SB_HEADS, 1), lambda b, s, pt: (0, 0)),
                  pl.BlockSpec((2 * SB_TK, 2 * SB_TK), lambda b, s, pt: (0, 0))]
        + [page_spec(g) for g in range(group)] * 2,
        out_specs=pl.BlockSpec((1, 1, SB_W), lambda b, s, pt: (b, 0, 0)),
        scratch_shapes=[pltpu.VMEM((SB_HEADS, SB_TK), F32), pltpu.VMEM((SB_HEADS, SB_W), F32)],
    )
    out = pl.pallas_call(
        functools.partial(_sb_decode_kernel, group=group),
        grid_spec=grid_spec,
        out_shape=jax.ShapeDtypeStruct((bs, 1, SB_W), F32),
        compiler_params=_cparams(("parallel", "arbitrary")),
        name="sb_decode",
    )(page_table.reshape(-1), q.reshape(bs, 1, SB_W), bias.reshape(SB_HEADS, 1), _suffix_matrix(),
      *([cache_kt] * group), *([cache_vt] * group))
    return out.reshape(bs, SB_W)


def _head_indicator(width, head):
    i = np.arange(width) // head
    return jnp.asarray(i[:, None] == i[None, :], BF16)


def _gdn_qkv_post(conv_out, seg):
    act = conv_out * _sigmoid(conv_out)
    q = act[:, :GDN_QK_W]
    k = act[:, GDN_QK_W:2 * GDN_QK_W]
    v = act[:, 2 * GDN_QK_W:]
    qn = q * lax.rsqrt(_dot_exact_rhs(q * q, seg) + L2_EPS) * (GDN_DK ** -0.5)
    kn = k * lax.rsqrt(_dot_exact_rhs(k * k, seg) + L2_EPS)
    return qn, kn, v


def _gdn_gates(ab, alog, dtb):
    g = -jnp.exp(alog) * _softplus(ab + dtb)
    return g, _sigmoid(ab)


def _gdn_pre_kernel(x_ref, halo_ref, ab_ref, cw_ref, alog_ref, dtb_ref, seg_ref, q_ref, k_ref, v_ref, gb_ref):
    i = pl.program_id(1)
    x = x_ref[0]
    tl = x.shape[0]
    halo = jnp.where(i > 0, halo_ref[0], 0.0)
    nh = halo.shape[0]
    ext = jnp.concatenate([halo, x], axis=0)
    out = None
    for tap in range(GDN_CONV):
        d = GDN_CONV - 1 - tap
        src = x if d == 0 else pltpu.roll(ext, d, axis=0)[nh:]
        term = src * cw_ref[tap:tap + 1, :]
        out = term if out is None else out + term
    qn, kn, v = _gdn_qkv_post(out, seg_ref[...])
    q_ref[0] = qn
    k_ref[0] = kn
    v_ref[0] = v

    g, beta = _gdn_gates(ab_ref[0], alog_ref[...], dtb_ref[...])
    tt = lax.broadcasted_iota(jnp.int32, g.shape, 0) % GDN_CHUNK
    k = 1
    while k < GDN_CHUNK:
        g = g + jnp.where(tt >= k, pltpu.roll(g, k, axis=0), 0.0)
        k *= 2
    lane = lax.broadcasted_iota(jnp.int32, g.shape, 1)
    gb_ref[0] = jnp.where(lane < GDN_HEADS, g, beta)


def gdn_pre(p3, conv_w, alog, dtb, *, tl):
    b, l, _ = p3.shape
    nh = 8
    out_sd = jax.ShapeDtypeStruct((b, l, GDN_QK_W), F32)
    return pl.pallas_call(
        _gdn_pre_kernel,
        grid=(b, l // tl),
        in_specs=[pl.BlockSpec((1, tl, GDN_QKV_W), lambda n, i: (n, i, P_GQKV // GDN_QKV_W)),
                  pl.BlockSpec((1, nh, GDN_QKV_W), lambda n, i: (n, jnp.maximum(i * (tl // nh) - 1, 0), 0)),
                  pl.BlockSpec((1, tl, LANES), lambda n, i: (n, i, P_AB // LANES)),
                  pl.BlockSpec((GDN_CONV, GDN_QKV_W), lambda n, i: (0, 0)),
                  pl.BlockSpec((1, LANES), lambda n, i: (0, 0)),
                  pl.BlockSpec((1, LANES), lambda n, i: (0, 0)),
                  pl.BlockSpec((GDN_QK_W, GDN_QK_W), lambda n, i: (0, 0))],
        out_specs=[pl.BlockSpec((1, tl, GDN_QK_W), lambda n, i: (n, i, 0))] * 3
        + [pl.BlockSpec((1, tl, LANES), lambda n, i: (n, i, 0))],
        out_shape=[out_sd, out_sd, out_sd, jax.ShapeDtypeStruct((b, l, LANES), F32)],
        compiler_params=_cparams(("parallel", "parallel")),
        name="gdn_pre",
    )(p3, p3, p3, conv_w, alog, dtb, _head_indicator(GDN_QK_W, GDN_DK))


def _gate_broadcast_matrix():
    r = np.arange(2 * LANES)[:, None] % LANES
    c = np.arange(2 * LANES)[None, :]
    return jnp.asarray(np.where(c < LANES, r < GDN_HEADS, (r >= GDN_HEADS) & (r < 2 * GDN_HEADS)), BF16)


def _gdn_chunk_terms(q, k, v, gb, e_mat):
    ch = GDN_CHUNK
    r = GDN_HEADS * ch
    sh = ch.bit_length() - 1
    chunks = range(len(q))
    ri = lax.broadcasted_iota(jnp.int32, (r, r), 0)
    ci = lax.broadcasted_iota(jnp.int32, (r, r), 1)
    same_head = (ri >> sh) == (ci >> sh)
    incl = same_head & (ri >= ci)
    strict = same_head & (ri > ci)
    eye = (ri == ci).astype(F32)

    def stack(x):
        return jnp.where(same_head, jnp.concatenate([x] * GDN_HEADS, axis=0), 0.0)

    def widen(x):
        return jnp.concatenate([x] * (r // LANES), axis=1)

    kbd, qbd, vbd = [stack(x) for x in k], [stack(x) for x in q], [stack(x) for x in v]

    l1 = lax.broadcasted_iota(jnp.int32, (r, LANES), 1)
    r1 = lax.broadcasted_iota(jnp.int32, (r, LANES), 0) >> sh
    gate_lane = (l1 == r1) | (l1 == r1 + GDN_HEADS)
    gsel = [_split(jnp.where(gate_lane, jnp.concatenate([x] * GDN_HEADS, axis=0), 0.0)) for x in gb]
    gbc = [_dot(jnp.concatenate([hi, lo], axis=1), e_mat) for hi, lo in gsel]
    g1 = [x[:, :LANES] for x in gbc]
    gl1 = [jnp.concatenate([jnp.broadcast_to(x[(h + 1) * ch - 1:(h + 1) * ch, :], (ch, LANES))
                            for h in range(GDN_HEADS)], axis=0) for x in g1]
    gcol = [widen(x) for x in g1]
    bcol = [widen(x[:, LANES:]) for x in gbc]
    eg = [widen(jnp.exp(x)) for x in g1]
    egl = [widen(jnp.exp(x)) for x in gl1]
    ekd = [widen(jnp.exp(gl1[i] - g1[i])) for i in chunks]

    decay = [jnp.where(incl, jnp.exp(jnp.where(incl, x - x.T, 0.0)), 0.0) for x in gcol]
    kb = [kbd[i] * bcol[i] for i in chunks]
    kbd_b = [x.astype(BF16) for x in kbd]
    a_mat = [jnp.where(strict, _dot_nt(kb[i].astype(BF16), kbd_b[i]) * decay[i], 0.0) for i in chunks]

    t_inv = [eye - jnp.where((ri >> 1) == (ci >> 1), x, 0.0) for x in a_mat]
    for lv in range(1, sh):
        off = ((ri >> (lv + 1)) == (ci >> (lv + 1))) & ((ri >> lv) != (ci >> lv))
        tb = [x.astype(BF16) for x in t_inv]
        m = [_dot(jnp.where(off, a_mat[i], 0.0).astype(BF16), tb[i]).astype(BF16) for i in chunks]
        t_inv = [t_inv[i] - _dot(tb[i], m[i]) for i in chunks]
    res = [(eye - t_inv[i]) - _mm3(a_mat[i], t_inv[i]) for i in chunks]
    t_inv = [t_inv[i] + _dot(t_inv[i].astype(BF16), res[i].astype(BF16)) for i in chunks]
    t_split = [_split(x) for x in t_inv]

    def apply_t(i, x):
        th, tl = t_split[i]
        xh, xl = _split(x)
        return _dot(th, xh) + (_dot(th, xl) + _dot(tl, xh))

    u = [apply_t(i, vbd[i] * bcol[i]) for i in chunks]
    w = [apply_t(i, kb[i] * eg[i]).astype(BF16) for i in chunks]
    qk = [jnp.where(incl, _dot_nt(qbd[i].astype(BF16), kbd_b[i]) * decay[i], 0.0).astype(BF16) for i in chunks]
    qd = [(qbd[i] * eg[i]).astype(BF16) for i in chunks]
    kd = [(kbd[i] * ekd[i]).astype(BF16) for i in chunks]
    return list(zip(u, w, qk, qd, kd, egl))


def _gdn_chunk_kernel(q_ref, k_ref, v_ref, gb_ref, z_ref, gn_ref, e_ref, seg_ref, og_ref, s_ref, st_ref, *, nc):
    c = pl.program_id(1)
    ch = GDN_CHUNK

    @pl.when(c == 0)
    def _():
        st_ref[...] = jnp.zeros_like(st_ref)

    rows = [slice(i * ch, (i + 1) * ch) for i in range(nc)]
    terms = _gdn_chunk_terms([q_ref[0, x, :] for x in rows], [k_ref[0, x, :] for x in rows],
                             [v_ref[0, x, :] for x in rows], [gb_ref[0, x, :] for x in rows], e_ref[...])

    s = st_ref[...]
    outs = []
    for u, w, qk, qd, kd, egl in terms:
        sb = s.astype(BF16)
        vnb = (u - _dot(w, sb)).astype(BF16)
        o = _dot(qd, sb) + _dot(qk, vnb)
        s = s * egl + _dot_tn(kd, vnb)
        oc = o[0:ch]
        for h in range(1, GDN_HEADS):
            oc = oc + o[h * ch:(h + 1) * ch]
        outs.append(oc)
    st_ref[...] = s

    oc = jnp.concatenate(outs, axis=0)
    ms = _dot_exact_rhs(oc * oc, seg_ref[...]) * (1.0 / GDN_DV)
    zz = z_ref[0]
    og_ref[0] = oc * lax.rsqrt(ms + RMS_EPS) * gn_ref[...] * (zz * _sigmoid(zz))

    @pl.when(c == pl.num_programs(1) - 1)
    def _():
        for h in range(GDN_HEADS):
            s_ref[0, h] = s[h * GDN_DK:(h + 1) * GDN_DK, h * GDN_DV:(h + 1) * GDN_DV]


def gdn_chunk(qn, kn, v, gb, p3, gnorm, *, nc):
    b, l, _ = qn.shape
    ch = GDN_CHUNK * nc
    r = GDN_HEADS * GDN_CHUNK
    blk = lambda w: pl.BlockSpec((1, ch, w), lambda n, c: (n, c, 0))
    return pl.pallas_call(
        functools.partial(_gdn_chunk_kernel, nc=nc),
        grid=(b, l // ch),
        in_specs=[blk(GDN_QK_W), blk(GDN_QK_W), blk(GDN_V_W), blk(LANES),
                  pl.BlockSpec((1, ch, GDN_V_W), lambda n, c: (n, c, P_GZ // GDN_V_W)),
                  pl.BlockSpec((1, GDN_V_W), lambda n, c: (0, 0)),
                  pl.BlockSpec((2 * LANES, 2 * LANES), lambda n, c: (0, 0)),
                  pl.BlockSpec((GDN_V_W, GDN_V_W), lambda n, c: (0, 0))],
        out_specs=[pl.BlockSpec((1, ch, GDN_V_W), lambda n, c: (n, c, 0)),
                   pl.BlockSpec((1, GDN_HEADS, GDN_DK, GDN_DV), lambda n, c: (n, 0, 0, 0))],
        out_shape=[jax.ShapeDtypeStruct((b, l, GDN_V_W), F32),
                   jax.ShapeDtypeStruct((b, GDN_HEADS, GDN_DK, GDN_DV), F32)],
        scratch_shapes=[pltpu.VMEM((r, r), F32)],
        compiler_params=_cparams(("parallel", "arbitrary")),
        name="gdn_chunk",
    )(qn, kn, v, gb, p3, jnp.tile(gnorm, GDN_HEADS).reshape(1, GDN_V_W), _gate_broadcast_matrix(),
      _head_indicator(GDN_V_W, GDN_DV))


def _sample_pre_kernel(up_ref, x_ref, ab_ref, ph_ref, ch_ref, pw_ref, psc_ref, cw_ref, alog_ref, dtb_ref, seg_ref,
                       op_ref, q_ref, k_ref, v_ref, gb_ref, *, pos0):
    u = up_ref[...]
    ph = ph_ref[...]
    hist = ph.shape[1]
    ti = lax.broadcasted_iota(jnp.int32, ph.shape, 1)
    grp = lax.broadcasted_iota(jnp.int32, u.shape, 1) // POOL_GC
    means = []
    for win in POOL_WINDOWS:
        n_hist = min(win - 1, hist)
        tail = jnp.sum(jnp.where(ti >= hist - n_hist, ph, 0.0), axis=1)
        means.append((tail + u) / float(min(win, pos0 + 1)))
    pooled = _pool_select(grp, means) - u
    op_ref[...] = _dot(pooled.astype(BF16), pw_ref[...]) * psc_ref[...]

    cw = cw_ref[...]
    out = jnp.sum(ch_ref[...] * cw[None, :GDN_CONV - 1, :], axis=1) + x_ref[...] * cw[GDN_CONV - 1:, :]
    qn, kn, v = _gdn_qkv_post(out, seg_ref[...])
    q_ref[...] = qn
    k_ref[...] = kn
    v_ref[...] = v

    g, beta = _gdn_gates(ab_ref[...], alog_ref[...], dtb_ref[...])
    lane = lax.broadcasted_iota(jnp.int32, g.shape, 1)
    gb_ref[...] = jnp.where(lane < GDN_HEADS, jnp.exp(g), beta)


def sample_pre(p, pool_hist, conv_hist, pool_w_bd, pool_scale, conv_w, alog, dtb, *, pos0):
    bs = p.shape[0]
    full = lambda shape: pl.BlockSpec(shape, lambda i: (0,) * len(shape))
    sd = lambda w: jax.ShapeDtypeStruct((bs, w), F32)
    return pl.pallas_call(
        functools.partial(_sample_pre_kernel, pos0=pos0),
        grid=(1,),
        in_specs=[pl.BlockSpec((bs, POOL_W), lambda i: (0, P_POOL // POOL_W)),
                  pl.BlockSpec((bs, GDN_QKV_W), lambda i: (0, P_GQKV // GDN_QKV_W)),
                  pl.BlockSpec((bs, LANES), lambda i: (0, P_AB // LANES)),
                  full(pool_hist.shape), full(conv_hist.shape), full((POOL_W, POOL_W)), full((1, POOL_W)),
                  full((GDN_CONV, GDN_QKV_W)), full((1, LANES)), full((1, LANES)), full((GDN_QK_W, GDN_QK_W))],
        out_specs=[full((bs, POOL_W)), full((bs, GDN_QK_W)), full((bs, GDN_QK_W)), full((bs, GDN_V_W)),
                   full((bs, LANES))],
        out_shape=[sd(POOL_W), sd(GDN_QK_W), sd(GDN_QK_W), sd(GDN_V_W), sd(LANES)],
        compiler_params=_cparams(("arbitrary",)),
        name="sample_pre",
    )(p, p, p, pool_hist, conv_hist, pool_w_bd, pool_scale.reshape(1, POOL_W), conv_w, alog, dtb,
      _head_indicator(GDN_QK_W, GDN_DK))


def _gdn_step_kernel(s_ref, k_ref, q_ref, v_ref, z_ref, eg_ref, beta_ref, gn_ref, so_ref, og_ref):
    k = k_ref[...]
    s = s_ref[...] * eg_ref[...]
    delta = (v_ref[...] - jnp.sum(k * s, axis=2, keepdims=True)) * beta_ref[...]
    s = s + k * delta
    so_ref[...] = s
    o = jnp.sum(q_ref[...] * s, axis=2, keepdims=True)
    zz = z_ref[...]
    og_ref[...] = o * _rms_scale(o) * gn_ref[...] * (zz * _sigmoid(zz))


def gdn_step(state, kn, qn, v, z, eg, beta, gnorm, *, bt):
    bs = state.shape[0]
    h, dk, dv = GDN_HEADS, GDN_DK, GDN_DV
    blk = lambda a, c: pl.BlockSpec((bt, h, a, c), lambda i: (i, 0, 0, 0))
    return pl.pallas_call(
        _gdn_step_kernel,
        grid=(bs // bt,),
        in_specs=[blk(dk, dv), blk(dk, 1), blk(dk, 1), blk(1, dv), blk(1, dv), blk(1, 1), blk(1, 1),
                  pl.BlockSpec((1, 1, 1, dv), lambda i: (0, 0, 0, 0))],
        out_specs=[blk(dk, dv), blk(1, dv)],
        out_shape=[jax.ShapeDtypeStruct((bs, h, dk, dv), F32), jax.ShapeDtypeStruct((bs, h, 1, dv), F32)],
        compiler_params=_cparams(("parallel",)),
        name="gdn_step",
    )(state, kn.reshape(bs, h, dk, 1), qn.reshape(bs, h, dk, 1), v.reshape(bs, h, 1, dv), z.reshape(bs, h, 1, dv),
      eg.reshape(bs, h, 1, 1), beta.reshape(bs, h, 1, 1), gnorm.reshape(1, 1, 1, dv))


def _largest_tile(n, cap):
    t = min(n, cap)
    while n % t:
        t //= 2
    return t


def _pack_w_in(w_in):
    depth, d, _ = w_in.shape
    pad = jnp.zeros((depth, d, P_GATE - P_AB - (R_GZ - R_GA)), w_in.dtype)
    return jnp.concatenate([w_in[..., R_GQKV:R_GA], w_in[..., R_POOL:R_GQKV], w_in[..., R_GZ:R_GATE],
                            w_in[..., R_GA:R_GZ], pad, w_in[..., R_GATE:R_END]], axis=-1).astype(BF16)


def _lane_pad(x):
    return jnp.pad(x, ((0, 0), (0, LANES - x.shape[1])))


def kernel(x_prompt, x_sample, cache_sb_k, cache_sb_v, page_table, state_pool, state_gdn_conv, state_gdn, norm1_g, w_in, pool_w, pool_scale, sb_bias, gdn_conv_w, gdn_a_log, gdn_dt_bias, gdn_norm_g, w_branch, w_o, norm2_g, w_ffn_in, w_ffn_out, normf_g):
    bp, seq, d = x_prompt.shape
    bs = x_sample.shape[0]
    depth = w_in.shape[0]
    n_phys, page = cache_sb_k.shape[1], cache_sb_k.shape[2]
    past_len = page_table.shape[1] * page
    tp = bp * seq

    w_in_p = _pack_w_in(w_in)
    w_branch_b = w_branch.astype(BF16)
    w_o_b = w_o.astype(BF16)
    w_ffn_in_b = w_ffn_in.astype(BF16)
    w_ffn_out_b = w_ffn_out.astype(BF16)
    eye_g = jnp.eye(len(POOL_WINDOWS), dtype=F32)
    pool_w_bd = jnp.einsum("lgcd,gh->lgchd", pool_w, eye_g).reshape(depth, POOL_W, POOL_W).astype(BF16)
    alog_p = _lane_pad(gdn_a_log)
    dtb_p = _lane_pad(gdn_dt_bias)
    cache_kt = jnp.transpose(cache_sb_k, (0, 1, 3, 4, 2)).reshape(depth, n_phys, SB_W, page)
    cache_vt = jnp.transpose(cache_sb_v, (0, 1, 3, 4, 2)).reshape(depth, n_phys, SB_W, page)
    n_pages = page_table.shape[1]
    page_group = _largest_tile(n_pages, 16)

    tm_p = _largest_tile(tp, 1024)
    tm_merge = _largest_tile(tp, 512)
    tm_s = _largest_tile(bs, 128)
    tq = _largest_tile(seq, 256)
    tl = _largest_tile(seq, 256)
    bt = _largest_tile(bs, 8)

    xp = x_prompt.reshape(tp, d)
    xs = x_sample.reshape(bs, d)
    outs = [[] for _ in range(10)]
    for l in range(depth):
        final = l == depth - 1
        p = rms_matmul(xp, norm1_g[l], w_in_p[l], tm=tm_p, tn=1024)
        p3 = p.reshape(bp, seq, P_W)
        o_pool = pool_prompt(p3, pool_w_bd[l], pool_scale[l])
        o_sb = sb_prompt(p3, sb_bias[l], tq=tq)
        qn, kn, gv, gb = gdn_pre(p3, gdn_conv_w[l], alog_p[l:l + 1], dtb_p[l:l + 1], tl=tl)
        o_g, s_p = gdn_chunk(qn, kn, gv, gb, p3, gdn_norm_g[l], nc=_largest_tile(seq // GDN_CHUNK, 4))
        xp = merge(xp, o_pool.reshape(tp, POOL_W), o_sb.reshape(tp, SB_W), o_g.reshape(tp, GDN_V_W), p,
                   w_branch_b[l], w_o_b[l], tm=tm_merge)
        xp = ffn(xp, norm2_g[l], w_ffn_in_b[l], w_ffn_out_b[l], normf_g, tm=tm_p, tf=256, final=final)
        outs[0].append(p3[..., P_SBK:P_SBV].reshape(bp, seq, SB_HEADS, SB_HEAD_DIM))
        outs[1].append(p3[..., P_SBV:P_GZ].reshape(bp, seq, SB_HEADS, SB_HEAD_DIM))
        outs[2].append(p3[:, seq - POOL_HIST:, P_POOL:P_SBQ])
        outs[3].append(p3[:, seq - (GDN_CONV - 1):, P_GQKV:P_POOL])
        outs[4].append(s_p)

        ps = rms_matmul(xs, norm1_g[l], w_in_p[l], tm=tm_s, tn=1024)
        o_pool_s, qn_s, kn_s, gv_s, gb_s = sample_pre(ps, state_pool[l], state_gdn_conv[l], pool_w_bd[l],
                                                      pool_scale[l], gdn_conv_w[l], alog_p[l:l + 1],
                                                      dtb_p[l:l + 1], pos0=past_len)
        o_sb_s = sb_decode(ps[:, P_SBQ:P_SBK], cache_kt, cache_vt, page_table, sb_bias[l], l, group=page_group)
        s_s, o_g_s = gdn_step(state_gdn[l], kn_s, qn_s, gv_s, ps[:, P_GZ:P_AB], gb_s[:, :GDN_HEADS],
                              gb_s[:, GDN_HEADS:2 * GDN_HEADS], gdn_norm_g[l], bt=bt)
        xs = merge(xs, o_pool_s, o_sb_s, o_g_s.reshape(bs, GDN_V_W), ps, w_branch_b[l], w_o_b[l], tm=tm_s)
        xs = ffn(xs, norm2_g[l], w_ffn_in_b[l], w_ffn_out_b[l], normf_g, tm=tm_s, tf=256, final=final)
        outs[5].append(ps[:, P_SBK:P_SBV].reshape(bs, 1, SB_HEADS, SB_HEAD_DIM))
        outs[6].append(ps[:, P_SBV:P_GZ].reshape(bs, 1, SB_HEADS, SB_HEAD_DIM))
        outs[7].append(jnp.concatenate([state_pool[l], ps[:, None, P_POOL:P_SBQ]], axis=1)[:, 1:])
        outs[8].append(jnp.concatenate([state_gdn_conv[l], ps[:, None, P_GQKV:P_POOL]], axis=1)[:, 1:])
        outs[9].append(s_s)

    y_prompt = xp.reshape(bp, seq, d)
    y_sample = xs.reshape(bs, 1, d)
    return (y_prompt, y_sample) + tuple(jnp.stack(o) for o in outs)
```

```python
import functools

import numpy as np
import jax
import jax.numpy as jnp
from jax import lax
from jax.experimental import pallas as pl
from jax.experimental.pallas import tpu as pltpu

F32 = jnp.float32
BF16 = jnp.bfloat16

D_MODEL = 1024
POOL_WINDOWS = (2, 4, 8, 16)
POOL_W = 256
POOL_GC = 64
POOL_HIST = 15
SB_HEADS = 8
SB_HEAD_DIM = 64
SB_W = 512
GDN_HEADS = 4
GDN_DK = 64
GDN_DV = 64
GDN_QK_W = 256
GDN_V_W = 256
GDN_QKV_W = 768
GDN_CONV = 4
GDN_CHUNK = 64
D_FF = 2816
RMS_EPS = 1e-6
L2_EPS = 1e-6

R_POOL, R_SBQ, R_GQKV, R_GA, R_GZ, R_GATE, R_END = 0, 256, 1792, 2560, 2568, 2824, 5896
P_GQKV, P_POOL, P_SBQ, P_SBK, P_SBV, P_GZ, P_AB, P_GATE, P_W = 0, 768, 1024, 1536, 2048, 2560, 2816, 3072, 6144

LANES = 128
SB_TK = LANES
VMEM_LIMIT = 56 * 1024 * 1024


def _cparams(sem):
    return pltpu.CompilerParams(dimension_semantics=sem, vmem_limit_bytes=VMEM_LIMIT)


def _sigmoid(x):
    return 1.0 / (1.0 + jnp.exp(-x))


def _softplus(x):
    return jnp.maximum(x, 0.0) + jnp.log(1.0 + jnp.exp(-jnp.abs(x)))


def _split(x):
    hi = x.astype(BF16)
    lo = (x - hi.astype(F32)).astype(BF16)
    return hi, lo


def _dot(a, b):
    return jnp.dot(a, b, preferred_element_type=F32)


def _dot_nt(a, b):
    return lax.dot_general(a, b, (((1,), (1,)), ((), ())), preferred_element_type=F32)


def _dot_tn(a, b):
    return lax.dot_general(a, b, (((0,), (0,)), ((), ())), preferred_element_type=F32)


def _mm3(a, b):
    ah, al = _split(a)
    bh, bl = _split(b)
    return _dot(ah, bh) + (_dot(ah, bl) + _dot(al, bh))


def _dot_exact_rhs(x, m):
    hi, lo = _split(x)
    return _dot(hi, m) + _dot(lo, m)


def _rms_scale(x):
    return lax.rsqrt(jnp.mean(x * x, axis=-1, keepdims=True) + RMS_EPS)


def _resident(shape):
    return pl.BlockSpec(shape, lambda *_: (0,) * len(shape), pipeline_mode=pl.Buffered(1))


def _rms_matmul_kernel(x_ref, g_ref, w_ref, o_ref, *, tn):
    x = x_ref[...]
    h = (x * _rms_scale(x) * g_ref[...]).astype(BF16)
    for j in range(w_ref.shape[1] // tn):
        o_ref[:, j * tn:(j + 1) * tn] = _dot(h, w_ref[:, j * tn:(j + 1) * tn])


def rms_matmul(x, g, w, *, tm, tn):
    t, d = x.shape
    n = w.shape[1]
    return pl.pallas_call(
        functools.partial(_rms_matmul_kernel, tn=tn),
        grid=(t // tm,),
        in_specs=[pl.BlockSpec((tm, d), lambda i: (i, 0)), _resident((1, d)), _resident((d, n))],
        out_specs=pl.BlockSpec((tm, n), lambda i: (i, 0)),
        out_shape=jax.ShapeDtypeStruct((t, n), F32),
        compiler_params=_cparams(("parallel",)),
        name="in_proj",
    )(x, g.reshape(1, d), w)


def _ffn_kernel(x_ref, g_ref, wi_ref, wo_ref, gf_ref, o_ref, act_ref, *, final, tf):
    x = x_ref[...]
    h = (x * _rms_scale(x) * g_ref[...]).astype(BF16)
    dff = wo_ref.shape[0]
    for f in range(dff // tf):
        gate = _dot(h, wi_ref[:, f * tf:(f + 1) * tf])
        up = _dot(h, wi_ref[:, dff + f * tf:dff + (f + 1) * tf])
        act_ref[:, f * tf:(f + 1) * tf] = ((gate * _sigmoid(gate)) * up).astype(BF16)
    y = x + _dot(act_ref[...], wo_ref[...])
    if final:
        y = y * _rms_scale(y) * gf_ref[...]
    o_ref[...] = y


def ffn(x, g, w_in, w_out, gf, *, tm, tf, final):
    t, d = x.shape
    dff = w_out.shape[0]
    return pl.pallas_call(
        functools.partial(_ffn_kernel, final=final, tf=tf),
        grid=(t // tm,),
        in_specs=[pl.BlockSpec((tm, d), lambda i: (i, 0)), _resident((1, d)), _resident((d, 2 * dff)),
                  _resident((dff, d)), _resident((1, d))],
        out_specs=pl.BlockSpec((tm, d), lambda i: (i, 0)),
        out_shape=jax.ShapeDtypeStruct((t, d), F32),
        scratch_shapes=[pltpu.VMEM((tm, dff), BF16)],
        compiler_params=_cparams(("parallel",)),
        name="ffn",
    )(x, g.reshape(1, d), w_in, w_out, gf.reshape(1, d))


def _merge_kernel(x_ref, op_ref, osb_ref, og_ref, g0_ref, g1_ref, g2_ref, wb_ref, wo_ref, o_ref):
    br_pool = _dot(op_ref[...].astype(BF16), wb_ref[0:POOL_W, :])
    br_sb = _dot(osb_ref[...].astype(BF16), wb_ref[POOL_W:POOL_W + SB_W, :])
    br_gdn = _dot(og_ref[...].astype(BF16), wb_ref[POOL_W + SB_W:, :])
    m = _sigmoid(g0_ref[...]) * br_pool + _sigmoid(g1_ref[...]) * br_sb + _sigmoid(g2_ref[...]) * br_gdn
    o_ref[...] = x_ref[...] + _dot(m.astype(BF16), wo_ref[...])


def merge(x, o_pool, o_sb, o_g, p, w_branch, w_o, *, tm):
    t, d = x.shape
    gblk = P_GATE // d
    return pl.pallas_call(
        _merge_kernel,
        grid=(t // tm,),
        in_specs=[pl.BlockSpec((tm, d), lambda i: (i, 0)),
                  pl.BlockSpec((tm, POOL_W), lambda i: (i, 0)),
                  pl.BlockSpec((tm, SB_W), lambda i: (i, 0)),
                  pl.BlockSpec((tm, GDN_V_W), lambda i: (i, 0)),
                  pl.BlockSpec((tm, d), lambda i: (i, gblk)),
                  pl.BlockSpec((tm, d), lambda i: (i, gblk + 1)),
                  pl.BlockSpec((tm, d), lambda i: (i, gblk + 2)),
                  _resident((d, d)), _resident((d, d))],
        out_specs=pl.BlockSpec((tm, d), lambda i: (i, 0)),
        out_shape=jax.ShapeDtypeStruct((t, d), F32),
        compiler_params=_cparams(("parallel",)),
        name="merge",
    )(x, o_pool, o_sb, o_g, p, p, p, w_branch, w_o)


def _pool_select(lane_grp, vals):
    out = vals[-1]
    for gi in range(len(vals) - 2, -1, -1):
        out = jnp.where(lane_grp == gi, vals[gi], out)
    return out


def _pool_prompt_kernel(u_ref, w_ref, sc_ref, o_ref):
    u = u_ref[0]
    t = lax.broadcasted_iota(jnp.int32, u.shape, 0)
    grp = lax.broadcasted_iota(jnp.int32, u.shape, 1) // POOL_GC

    def shifted(x, k):
        return jnp.where(t >= k, pltpu.roll(x, k, axis=0), 0.0)

    sums = []
    s, w = u, 1
    for win in POOL_WINDOWS:
        while w < win:
            s = s + shifted(s, w)
            w *= 2
        sums.append(s)
    sel = _pool_select(grp, sums)
    win = _pool_select(grp, [jnp.full(u.shape, wn, jnp.int32) for wn in POOL_WINDOWS])
    cnt = jnp.minimum(win, t + 1).astype(F32)
    pooled = sel / cnt - u
    o_ref[0] = _dot(pooled.astype(BF16), w_ref[...]) * sc_ref[...]


def pool_prompt(p3, w_bd, scale):
    b, l, _ = p3.shape
    return pl.pallas_call(
        _pool_prompt_kernel,
        grid=(b,),
        in_specs=[pl.BlockSpec((1, l, POOL_W), lambda i: (i, 0, P_POOL // POOL_W)),
                  pl.BlockSpec((POOL_W, POOL_W), lambda i: (0, 0)),
                  pl.BlockSpec((1, POOL_W), lambda i: (0, 0))],
        out_specs=pl.BlockSpec((1, l, POOL_W), lambda i: (i, 0, 0)),
        out_shape=jax.ShapeDtypeStruct((b, l, POOL_W), F32),
        compiler_params=_cparams(("parallel",)),
        name="pool_prompt",
    )(p3, w_bd, scale.reshape(1, POOL_W))


def _sb_log_terms(z, vis, suffix_mat):
    sp = _softplus(z)
    hi, lo = _split(sp if vis is None else jnp.where(vis, sp, 0.0))
    return sp, _dot(jnp.concatenate([hi, lo], axis=1), suffix_mat)


def _head_pair_rows(x, first_head_lanes):
    return jnp.concatenate([jnp.where(first_head_lanes, x, 0.0), jnp.where(first_head_lanes, 0.0, x)],
                           axis=0).astype(BF16)


def _sb_prompt_kernel(bias_ref, q_ref, k_ref, v_ref, u2_ref, o_ref, qb_ref, acc_ref, c_ref, *, tq):
    qi = pl.program_id(1)
    tk = SB_TK
    hw = 2 * SB_HEAD_DIM
    npair = SB_HEADS // 2
    n_diag = tq // tk
    head0 = lax.broadcasted_iota(jnp.int32, (tk, hw), 1) < SB_HEAD_DIM
    row = lax.broadcasted_iota(jnp.int32, (tq, tk), 0)
    col = lax.broadcasted_iota(jnp.int32, (tq, tk), 1)
    qb_ref[...] = (q_ref[0] * (SB_HEAD_DIM ** -0.5)).astype(BF16)
    acc_ref[...] = jnp.zeros_like(acc_ref)
    c_ref[...] = jnp.zeros_like(c_ref)
    suffix_mat = u2_ref[...]

    pairs = [slice(p * hw, (p + 1) * hw) for p in range(npair)]

    def key_blocks(blocks):
        z2 = [[_dot_nt(qb_ref[:, lanes], _head_pair_rows(k_ref[0, pl.ds(ks, tk), lanes], head0)) for lanes in pairs]
              for ks, _ in blocks]
        z = [[zb[h // 2][:, (h % 2) * tk:(h % 2 + 1) * tk] + bias_ref[h] for h in range(SB_HEADS)] for zb in z2]
        terms = [[_sb_log_terms(zh, vis, suffix_mat) for zh in zb] for zb, (_, vis) in zip(z, blocks)]
        a = [[None] * SB_HEADS for _ in blocks]
        for h in range(SB_HEADS):
            c = c_ref[h]
            for i, (_, vis) in enumerate(blocks):
                sp, s2 = terms[i][h]
                ah = jnp.exp(z[i][h] - sp - s2[:, :tk] - c)
                a[i][h] = (ah if vis is None else jnp.where(vis, ah, 0.0)).astype(BF16)
                c = c + s2[:, tk:]
            c_ref[h] = c
        for p, lanes in enumerate(pairs):
            upd = None
            for i, (ks, _) in enumerate(blocks):
                vv = _head_pair_rows(v_ref[0, pl.ds(ks, tk), lanes], head0)
                d = _dot(jnp.concatenate(a[i][2 * p:2 * p + 2], axis=1), vv)
                upd = d if upd is None else upd + d
            acc_ref[:, lanes] += upd

    key_blocks([(pl.multiple_of(qi * tq + d * tk, tk), (col + d * tk) < row) for d in range(n_diag - 1, -1, -1)])

    def full_body(jj, carry):
        last = qi * n_diag - 1 - jj * n_diag
        key_blocks([(pl.multiple_of((last - d) * tk, tk), None) for d in range(n_diag)])
        return carry

    lax.fori_loop(0, qi, full_body, 0)
    o_ref[0] = acc_ref[...]


def _suffix_matrix():
    j = np.arange(2 * SB_TK)[:, None] % SB_TK
    c = np.arange(2 * SB_TK)[None, :]
    return jnp.asarray(np.where(c < SB_TK, j > c, True), BF16)


def sb_prompt(p3, bias, *, tq):
    b, l, _ = p3.shape
    return pl.pallas_call(
        functools.partial(_sb_prompt_kernel, tq=tq),
        grid=(b, l // tq),
        in_specs=[pl.BlockSpec(memory_space=pltpu.SMEM),
                  pl.BlockSpec((1, tq, SB_W), lambda i, q: (i, q, P_SBQ // SB_W)),
                  pl.BlockSpec((1, l, SB_W), lambda i, q: (i, 0, P_SBK // SB_W)),
                  pl.BlockSpec((1, l, SB_W), lambda i, q: (i, 0, P_SBV // SB_W)),
                  pl.BlockSpec((2 * SB_TK, 2 * SB_TK), lambda i, q: (0, 0))],
        out_specs=pl.BlockSpec((1, tq, SB_W), lambda i, q: (i, q, 0)),
        out_shape=jax.ShapeDtypeStruct((b, l, SB_W), F32),
        scratch_shapes=[pltpu.VMEM((tq, SB_W), BF16), pltpu.VMEM((tq, SB_W), F32),
                        pltpu.VMEM((SB_HEADS, tq, SB_TK), F32)],
        compiler_params=_cparams(("parallel", "arbitrary")),
        name="sb_prompt",
    )(bias, p3, p3, p3, _suffix_matrix())


def _sb_decode_kernel(pt_ref, q_ref, bias_ref, u2_ref, *refs, group):
    k_refs, v_refs = refs[:group], refs[group:2 * group]
    o_ref, c_ref, acc_ref = refs[2 * group:]
    step = pl.program_id(1)
    nh = SB_HEADS
    own = (lax.broadcasted_iota(jnp.int32, (nh, SB_W), 0)
           == lax.broadcasted_iota(jnp.int32, (nh, SB_W), 1) // SB_HEAD_DIM)

    @pl.when(step == 0)
    def _():
        c_ref[...] = jnp.zeros_like(c_ref)
        acc_ref[...] = jnp.zeros_like(acc_ref)

    q = jnp.broadcast_to(q_ref[0] * (SB_HEAD_DIM ** -0.5), (nh, SB_W))
    qbd = jnp.where(own, q, 0.0).astype(BF16)
    z = jnp.concatenate([_dot(qbd, k_refs[g][...].astype(BF16)) + bias_ref[...] for g in range(group)], axis=0)
    sp, s2 = _sb_log_terms(z, None, u2_ref[...])
    later = [None] * group
    run = c_ref[...]
    for g in range(group - 1, -1, -1):
        later[g] = run
        run = run + s2[g * nh:(g + 1) * nh, SB_TK:]
    c_ref[...] = run
    a = jnp.exp(z - sp - s2[:, :SB_TK] - jnp.concatenate(later, axis=0))
    acc = acc_ref[...]
    for g in range(group):
        acc = acc + _dot_nt(a[g * nh:(g + 1) * nh].astype(BF16), v_refs[g][...].astype(BF16))
    acc_ref[...] = acc

    @pl.when(step == pl.num_programs(1) - 1)
    def _():
        o_ref[0] = jnp.sum(jnp.where(own, acc, 0.0), axis=0, keepdims=True)


def sb_decode(q, cache_kt, cache_vt, page_table, bias, layer, *, group):
    bs, n_pages = page_table.shape
    page = cache_kt.shape[3]
    assert page == SB_TK and n_pages % group == 0
    n_steps = n_pages // group

    def page_spec(g):
        def index(b, s, pt):
            return (layer, pt[b * n_pages + (n_steps - 1 - s) * group + g], 0, 0)
        return pl.BlockSpec((None, None, SB_W, page), index)

    grid_spec = pltpu.PrefetchScalarGridSpec(
        num_scalar_prefetch=1,
        grid=(bs, n_steps),
        in_specs=[pl.BlockSpec((1, 1, SB_W), lambda b, s, pt: (b, 0, 0)),
                  pl.BlockSpec((SB_HEADS, 1), lambda b, s, pt: (0, 0)),
                  pl.BlockSpec((2 * SB_TK, 2 * SB_TK), lambda b, s, pt: (0, 0))]
        + [page_spec(g) for g in range(group)] * 2,
        out_specs=pl.BlockSpec((1, 1, SB_W), lambda b, s, pt: (b, 0, 0)),
        scratch_shapes=[pltpu.VMEM((SB_HEADS, SB_TK), F32), pltpu.VMEM((SB_HEADS, SB_W), F32)],
    )
    out = pl.pallas_call(
        functools.partial(_sb_decode_kernel, group=group),
        grid_spec=grid_spec,
        out_shape=jax.ShapeDtypeStruct((bs, 1, SB_W), F32),
        compiler_params=_cparams(("parallel", "arbitrary")),
        name="sb_decode",
    )(page_table.reshape(-1), q.reshape(bs, 1, SB_W), bias.reshape(SB_HEADS, 1), _suffix_matrix(),
      *([cache_kt] * group), *([cache_vt] * group))
    return out.reshape(bs, SB_W)


def _head_indicator(width, head):
    i = np.arange(width) // head
    return jnp.asarray(i[:, None] == i[None, :], BF16)


def _gdn_qkv_post(conv_out, seg):
    act = conv_out * _sigmoid(conv_out)
    q = act[:, :GDN_QK_W]
    k = act[:, GDN_QK_W:2 * GDN_QK_W]
    v = act[:, 2 * GDN_QK_W:]
    qn = q * lax.rsqrt(_dot_exact_rhs(q * q, seg) + L2_EPS) * (GDN_DK ** -0.5)
    kn = k * lax.rsqrt(_dot_exact_rhs(k * k, seg) + L2_EPS)
    return qn, kn, v


def _gdn_gates(ab, alog, dtb):
    g = -jnp.exp(alog) * _softplus(ab + dtb)
    return g, _sigmoid(ab)


def _gdn_pre_kernel(x_ref, halo_ref, ab_ref, cw_ref, alog_ref, dtb_ref, seg_ref, q_ref, k_ref, v_ref, gb_ref):
    i = pl.program_id(1)
    x = x_ref[0]
    tl = x.shape[0]
    halo = jnp.where(i > 0, halo_ref[0], 0.0)
    nh = halo.shape[0]
    ext = jnp.concatenate([halo, x], axis=0)
    out = None
    for tap in range(GDN_CONV):
        d = GDN_CONV - 1 - tap
        src = x if d == 0 else pltpu.roll(ext, d, axis=0)[nh:]
        term = src * cw_ref[tap:tap + 1, :]
        out = term if out is None else out + term
    qn, kn, v = _gdn_qkv_post(out, seg_ref[...])
    q_ref[0] = qn
    k_ref[0] = kn
    v_ref[0] = v

    g, beta = _gdn_gates(ab_ref[0], alog_ref[...], dtb_ref[...])
    tt = lax.broadcasted_iota(jnp.int32, g.shape, 0) % GDN_CHUNK
    k = 1
    while k < GDN_CHUNK:
        g = g + jnp.where(tt >= k, pltpu.roll(g, k, axis=0), 0.0)
        k *= 2
    lane = lax.broadcasted_iota(jnp.int32, g.shape, 1)
    gb_ref[0] = jnp.where(lane < GDN_HEADS, g, beta)


def gdn_pre(p3, conv_w, alog, dtb, *, tl):
    b, l, _ = p3.shape
    nh = 8
    out_sd = jax.ShapeDtypeStruct((b, l, GDN_QK_W), F32)
    return pl.pallas_call(
        _gdn_pre_kernel,
        grid=(b, l // tl),
        in_specs=[pl.BlockSpec((1, tl, GDN_QKV_W), lambda n, i: (n, i, P_GQKV // GDN_QKV_W)),
                  pl.BlockSpec((1, nh, GDN_QKV_W), lambda n, i: (n, jnp.maximum(i * (tl // nh) - 1, 0), 0)),
                  pl.BlockSpec((1, tl, LANES), lambda n, i: (n, i, P_AB // LANES)),
                  pl.BlockSpec((GDN_CONV, GDN_QKV_W), lambda n, i: (0, 0)),
                  pl.BlockSpec((1, LANES), lambda n, i: (0, 0)),
                  pl.BlockSpec((1, LANES), lambda n, i: (0, 0)),
                  pl.BlockSpec((GDN_QK_W, GDN_QK_W), lambda n, i: (0, 0))],
        out_specs=[pl.BlockSpec((1, tl, GDN_QK_W), lambda n, i: (n, i, 0))] * 3
        + [pl.BlockSpec((1, tl, LANES), lambda n, i: (n, i, 0))],
        out_shape=[out_sd, out_sd, out_sd, jax.ShapeDtypeStruct((b, l, LANES), F32)],
        compiler_params=_cparams(("parallel", "parallel")),
        name="gdn_pre",
    )(p3, p3, p3, conv_w, alog, dtb, _head_indicator(GDN_QK_W, GDN_DK))


def _gate_broadcast_matrix():
    r = np.arange(2 * LANES)[:, None] % LANES
    c = np.arange(2 * LANES)[None, :]
    return jnp.asarray(np.where(c < LANES, r < GDN_HEADS, (r >= GDN_HEADS) & (r < 2 * GDN_HEADS)), BF16)


def _gdn_chunk_terms(q, k, v, gb, e_mat):
    ch = GDN_CHUNK
    r = GDN_HEADS * ch
    sh = ch.bit_length() - 1
    chunks = range(len(q))
    ri = lax.broadcasted_iota(jnp.int32, (r, r), 0)
    ci = lax.broadcasted_iota(jnp.int32, (r, r), 1)
    same_head = (ri >> sh) == (ci >> sh)
    incl = same_head & (ri >= ci)
    strict = same_head & (ri > ci)
    eye = (ri == ci).astype(F32)

    def stack(x):
        return jnp.where(same_head, jnp.concatenate([x] * GDN_HEADS, axis=0), 0.0)

    def widen(x):
        return jnp.concatenate([x] * (r // LANES), axis=1)

    kbd, qbd, vbd = [stack(x) for x in k], [stack(x) for x in q], [stack(x) for x in v]
    yield

    l1 = lax.broadcasted_iota(jnp.int32, (r, LANES), 1)
    r1 = lax.broadcasted_iota(jnp.int32, (r, LANES), 0) >> sh
    gate_lane = (l1 == r1) | (l1 == r1 + GDN_HEADS)
    gsel = [_split(jnp.where(gate_lane, jnp.concatenate([x] * GDN_HEADS, axis=0), 0.0)) for x in gb]
    gbc = [_dot(jnp.concatenate([hi, lo], axis=1), e_mat) for hi, lo in gsel]
    g1 = [x[:, :LANES] for x in gbc]
    gl1 = [jnp.concatenate([jnp.broadcast_to(x[(h + 1) * ch - 1:(h + 1) * ch, :], (ch, LANES))
                            for h in range(GDN_HEADS)], axis=0) for x in g1]
    gcol = [widen(x) for x in g1]
    bcol = [widen(x[:, LANES:]) for x in gbc]
    eg = [widen(jnp.exp(x)) for x in g1]
    egl = [widen(jnp.exp(x)) for x in gl1]
    ekd = [widen(jnp.exp(gl1[i] - g1[i])) for i in chunks]

    yield
    decay = [jnp.where(incl, jnp.exp(jnp.where(incl, x - x.T, 0.0)), 0.0) for x in gcol]
    kb = [kbd[i] * bcol[i] for i in chunks]
    kbd_b = [x.astype(BF16) for x in kbd]
    a_mat = [jnp.where(strict, _dot_nt(kb[i].astype(BF16), kbd_b[i]) * decay[i], 0.0) for i in chunks]
    yield

    t_inv = [eye - jnp.where((ri >> 1) == (ci >> 1), x, 0.0) for x in a_mat]
    for lv in range(1, sh):
        off = ((ri >> (lv + 1)) == (ci >> (lv + 1))) & ((ri >> lv) != (ci >> lv))
        tb = [x.astype(BF16) for x in t_inv]
        m = [_dot(jnp.where(off, a_mat[i], 0.0).astype(BF16), tb[i]).astype(BF16) for i in chunks]
        yield
        t_inv = [t_inv[i] - _dot(tb[i], m[i]) for i in chunks]
        yield
    res = [(eye - t_inv[i]) - _mm3(a_mat[i], t_inv[i]) for i in chunks]
    yield
    t_inv = [t_inv[i] + _dot(t_inv[i].astype(BF16), res[i].astype(BF16)) for i in chunks]
    t_split = [_split(x) for x in t_inv]
    yield

    def apply_t(i, x):
        th, tl = t_split[i]
        xh, xl = _split(x)
        return _dot(th, xh) + (_dot(th, xl) + _dot(tl, xh))

    u = [apply_t(i, vbd[i] * bcol[i]) for i in chunks]
    yield
    w = [apply_t(i, kb[i] * eg[i]).astype(BF16) for i in chunks]
    yield
    qk = [jnp.where(incl, _dot_nt(qbd[i].astype(BF16), kbd_b[i]) * decay[i], 0.0).astype(BF16) for i in chunks]
    qd = [(qbd[i] * eg[i]).astype(BF16) for i in chunks]
    kd = [(kbd[i] * ekd[i]).astype(BF16) for i in chunks]
    return list(zip(u, w, qk, qd, kd, egl))


def _run_stages(stages, between=()):
    between = list(between)
    while True:
        try:
            next(stages)
        except StopIteration as done:
            result = done.value
            break
        if between:
            between.pop(0)()
    for step in between:
        step()
    return result


def _gdn_chunk_kernel(q_ref, k_ref, v_ref, gb_ref, z_ref, gn_ref, e_ref, seg_ref, og_ref, s_ref, st_ref, *, nc):
    c = pl.program_id(1)
    ch = GDN_CHUNK

    @pl.when(c == 0)
    def _():
        st_ref[...] = jnp.zeros_like(st_ref)

    def terms_of(chunk_ids):
        rows = [slice(i * ch, (i + 1) * ch) for i in chunk_ids]
        return _gdn_chunk_terms([q_ref[0, x, :] for x in rows], [k_ref[0, x, :] for x in rows],
                                [v_ref[0, x, :] for x in rows], [gb_ref[0, x, :] for x in rows], e_ref[...])

    state = {"s": st_ref[...]}
    outs = []

    def recurrence(terms):
        steps = []
        for u, w, qk, qd, kd, egl in terms:
            def first(u=u, w=w, qd=qd):
                sb = state["s"].astype(BF16)
                state["vnb"] = (u - _dot(w, sb)).astype(BF16)
                state["o"] = _dot(qd, sb)

            def second(qk=qk, kd=kd, egl=egl):
                o = state["o"] + _dot(qk, state["vnb"])
                state["s"] = state["s"] * egl + _dot_tn(kd, state["vnb"])
                oc = o[0:ch]
                for h in range(1, GDN_HEADS):
                    oc = oc + o[h * ch:(h + 1) * ch]
                outs.append(oc)
            steps += [first, second]
        return steps

    half = max(nc // 2, 1)
    terms_a = _run_stages(terms_of(range(half)))
    if nc > half:
        terms_b = _run_stages(terms_of(range(half, nc)), recurrence(terms_a))
        _run_stages(iter(()), recurrence(terms_b))
    else:
        _run_stages(iter(()), recurrence(terms_a))
    s = state["s"]
    st_ref[...] = s

    oc = jnp.concatenate(outs, axis=0)
    ms = _dot_exact_rhs(oc * oc, seg_ref[...]) * (1.0 / GDN_DV)
    zz = z_ref[0]
    og_ref[0] = oc * lax.rsqrt(ms + RMS_EPS) * gn_ref[...] * (zz * _sigmoid(zz))

    @pl.when(c == pl.num_programs(1) - 1)
    def _():
        for h in range(GDN_HEADS):
            s_ref[0, h] = s[h * GDN_DK:(h + 1) * GDN_DK, h * GDN_DV:(h + 1) * GDN_DV]


def gdn_chunk(qn, kn, v, gb, p3, gnorm, *, nc):
    b, l, _ = qn.shape
    ch = GDN_CHUNK * nc
    r = GDN_HEADS * GDN_CHUNK
    blk = lambda w: pl.BlockSpec((1, ch, w), lambda n, c: (n, c, 0))
    return pl.pallas_call(
        functools.partial(_gdn_chunk_kernel, nc=nc),
        grid=(b, l // ch),
        in_specs=[blk(GDN_QK_W), blk(GDN_QK_W), blk(GDN_V_W), blk(LANES),
                  pl.BlockSpec((1, ch, GDN_V_W), lambda n, c: (n, c, P_GZ // GDN_V_W)),
                  pl.BlockSpec((1, GDN_V_W), lambda n, c: (0, 0)),
                  pl.BlockSpec((2 * LANES, 2 * LANES), lambda n, c: (0, 0)),
                  pl.BlockSpec((GDN_V_W, GDN_V_W), lambda n, c: (0, 0))],
        out_specs=[pl.BlockSpec((1, ch, GDN_V_W), lambda n, c: (n, c, 0)),
                   pl.BlockSpec((1, GDN_HEADS, GDN_DK, GDN_DV), lambda n, c: (n, 0, 0, 0))],
        out_shape=[jax.ShapeDtypeStruct((b, l, GDN_V_W), F32),
                   jax.ShapeDtypeStruct((b, GDN_HEADS, GDN_DK, GDN_DV), F32)],
        scratch_shapes=[pltpu.VMEM((r, r), F32)],
        compiler_params=_cparams(("parallel", "arbitrary")),
        name="gdn_chunk",
    )(qn, kn, v, gb, p3, jnp.tile(gnorm, GDN_HEADS).reshape(1, GDN_V_W), _gate_broadcast_matrix(),
      _head_indicator(GDN_V_W, GDN_DV))


def _sample_pre_kernel(up_ref, x_ref, ab_ref, ph_ref, ch_ref, pw_ref, psc_ref, cw_ref, alog_ref, dtb_ref, seg_ref,
                       op_ref, q_ref, k_ref, v_ref, gb_ref, *, pos0):
    u = up_ref[...]
    ph = ph_ref[...]
    hist = ph.shape[1]
    ti = lax.broadcasted_iota(jnp.int32, ph.shape, 1)
    grp = lax.broadcasted_iota(jnp.int32, u.shape, 1) // POOL_GC
    means = []
    for win in POOL_WINDOWS:
        n_hist = min(win - 1, hist)
        tail = jnp.sum(jnp.where(ti >= hist - n_hist, ph, 0.0), axis=1)
        means.append((tail + u) / float(min(win, pos0 + 1)))
    pooled = _pool_select(grp, means) - u
    op_ref[...] = _dot(pooled.astype(BF16), pw_ref[...]) * psc_ref[...]

    cw = cw_ref[...]
    out = jnp.sum(ch_ref[...] * cw[None, :GDN_CONV - 1, :], axis=1) + x_ref[...] * cw[GDN_CONV - 1:, :]
    qn, kn, v = _gdn_qkv_post(out, seg_ref[...])
    q_ref[...] = qn
    k_ref[...] = kn
    v_ref[...] = v

    g, beta = _gdn_gates(ab_ref[...], alog_ref[...], dtb_ref[...])
    lane = lax.broadcasted_iota(jnp.int32, g.shape, 1)
    gb_ref[...] = jnp.where(lane < GDN_HEADS, jnp.exp(g), beta)


def sample_pre(p, pool_hist, conv_hist, pool_w_bd, pool_scale, conv_w, alog, dtb, *, pos0):
    bs = p.shape[0]
    full = lambda shape: pl.BlockSpec(shape, lambda i: (0,) * len(shape))
    sd = lambda w: jax.ShapeDtypeStruct((bs, w), F32)
    return pl.pallas_call(
        functools.partial(_sample_pre_kernel, pos0=pos0),
        grid=(1,),
        in_specs=[pl.BlockSpec((bs, POOL_W), lambda i: (0, P_POOL // POOL_W)),
                  pl.BlockSpec((bs, GDN_QKV_W), lambda i: (0, P_GQKV // GDN_QKV_W)),
                  pl.BlockSpec((bs, LANES), lambda i: (0, P_AB // LANES)),
                  full(pool_hist.shape), full(conv_hist.shape), full((POOL_W, POOL_W)), full((1, POOL_W)),
                  full((GDN_CONV, GDN_QKV_W)), full((1, LANES)), full((1, LANES)), full((GDN_QK_W, GDN_QK_W))],
        out_specs=[full((bs, POOL_W)), full((bs, GDN_QK_W)), full((bs, GDN_QK_W)), full((bs, GDN_V_W)),
                   full((bs, LANES))],
        out_shape=[sd(POOL_W), sd(GDN_QK_W), sd(GDN_QK_W), sd(GDN_V_W), sd(LANES)],
        compiler_params=_cparams(("arbitrary",)),
        name="sample_pre",
    )(p, p, p, pool_hist, conv_hist, pool_w_bd, pool_scale.reshape(1, POOL_W), conv_w, alog, dtb,
      _head_indicator(GDN_QK_W, GDN_DK))


def _gdn_step_kernel(s_ref, k_ref, q_ref, v_ref, z_ref, eg_ref, beta_ref, gn_ref, so_ref, og_ref):
    k = k_ref[...]
    s = s_ref[...] * eg_ref[...]
    delta = (v_ref[...] - jnp.sum(k * s, axis=2, keepdims=True)) * beta_ref[...]
    s = s + k * delta
    so_ref[...] = s
    o = jnp.sum(q_ref[...] * s, axis=2, keepdims=True)
    zz = z_ref[...]
    og_ref[...] = o * _rms_scale(o) * gn_ref[...] * (zz * _sigmoid(zz))


def gdn_step(state, kn, qn, v, z, eg, beta, gnorm, *, bt):
    bs = state.shape[0]
    h, dk, dv = GDN_HEADS, GDN_DK, GDN_DV
    blk = lambda a, c: pl.BlockSpec((bt, h, a, c), lambda i: (i, 0, 0, 0))
    return pl.pallas_call(
        _gdn_step_kernel,
        grid=(bs // bt,),
        in_specs=[blk(dk, dv), blk(dk, 1), blk(dk, 1), blk(1, dv), blk(1, dv), blk(1, 1), blk(1, 1),
                  pl.BlockSpec((1, 1, 1, dv), lambda i: (0, 0, 0, 0))],
        out_specs=[blk(dk, dv), blk(1, dv)],
        out_shape=[jax.ShapeDtypeStruct((bs, h, dk, dv), F32), jax.ShapeDtypeStruct((bs, h, 1, dv), F32)],
        compiler_params=_cparams(("parallel",)),
        name="gdn_step",
    )(state, kn.reshape(bs, h, dk, 1), qn.reshape(bs, h, dk, 1), v.reshape(bs, h, 1, dv), z.reshape(bs, h, 1, dv),
      eg.reshape(bs, h, 1, 1), beta.reshape(bs, h, 1, 1), gnorm.reshape(1, 1, 1, dv))


def _largest_tile(n, cap):
    t = min(n, cap)
    while n % t:
        t //= 2
    return t


def _pack_w_in(w_in):
    depth, d, _ = w_in.shape
    pad = jnp.zeros((depth, d, P_GATE - P_AB - (R_GZ - R_GA)), w_in.dtype)
    return jnp.concatenate([w_in[..., R_GQKV:R_GA], w_in[..., R_POOL:R_GQKV], w_in[..., R_GZ:R_GATE],
                            w_in[..., R_GA:R_GZ], pad, w_in[..., R_GATE:R_END]], axis=-1).astype(BF16)


def _lane_pad(x):
    return jnp.pad(x, ((0, 0), (0, LANES - x.shape[1])))


def kernel(x_prompt, x_sample, cache_sb_k, cache_sb_v, page_table, state_pool, state_gdn_conv, state_gdn, norm1_g, w_in, pool_w, pool_scale, sb_bias, gdn_conv_w, gdn_a_log, gdn_dt_bias, gdn_norm_g, w_branch, w_o, norm2_g, w_ffn_in, w_ffn_out, normf_g):
    bp, seq, d = x_prompt.shape
    bs = x_sample.shape[0]
    depth = w_in.shape[0]
    n_phys, page = cache_sb_k.shape[1], cache_sb_k.shape[2]
    past_len = page_table.shape[1] * page
    tp = bp * seq

    w_in_p = _pack_w_in(w_in)
    w_branch_b = w_branch.astype(BF16)
    w_o_b = w_o.astype(BF16)
    w_ffn_in_b = w_ffn_in.astype(BF16)
    w_ffn_out_b = w_ffn_out.astype(BF16)
    eye_g = jnp.eye(len(POOL_WINDOWS), dtype=F32)
    pool_w_bd = jnp.einsum("lgcd,gh->lgchd", pool_w, eye_g).reshape(depth, POOL_W, POOL_W).astype(BF16)
    alog_p = _lane_pad(gdn_a_log)
    dtb_p = _lane_pad(gdn_dt_bias)
    cache_kt = jnp.transpose(cache_sb_k, (0, 1, 3, 4, 2)).reshape(depth, n_phys, SB_W, page)
    cache_vt = jnp.transpose(cache_sb_v, (0, 1, 3, 4, 2)).reshape(depth, n_phys, SB_W, page)
    n_pages = page_table.shape[1]
    page_group = _largest_tile(n_pages, 16)

    tm_p = _largest_tile(tp, 512)
    tm_merge = _largest_tile(tp, 512)
    tm_s = _largest_tile(bs, 128)
    tq = _largest_tile(seq, 256)
    tl = _largest_tile(seq, 256)
    bt = _largest_tile(bs, 8)

    xp = x_prompt.reshape(tp, d)
    xs = x_sample.reshape(bs, d)
    outs = [[] for _ in range(10)]
    for l in range(depth):
        final = l == depth - 1
        p = rms_matmul(xp, norm1_g[l], w_in_p[l], tm=tm_p, tn=1024)
        p3 = p.reshape(bp, seq, P_W)
        o_pool = pool_prompt(p3, pool_w_bd[l], pool_scale[l])
        o_sb = sb_prompt(p3, sb_bias[l], tq=tq)
        qn, kn, gv, gb = gdn_pre(p3, gdn_conv_w[l], alog_p[l:l + 1], dtb_p[l:l + 1], tl=tl)
        o_g, s_p = gdn_chunk(qn, kn, gv, gb, p3, gdn_norm_g[l], nc=_largest_tile(seq // GDN_CHUNK, 8))
        xp = merge(xp, o_pool.reshape(tp, POOL_W), o_sb.reshape(tp, SB_W), o_g.reshape(tp, GDN_V_W), p,
                   w_branch_b[l], w_o_b[l], tm=tm_merge)
        xp = ffn(xp, norm2_g[l], w_ffn_in_b[l], w_ffn_out_b[l], normf_g, tm=tm_p, tf=256, final=final)
        outs[0].append(p3[..., P_SBK:P_SBV].reshape(bp, seq, SB_HEADS, SB_HEAD_DIM))
        outs[1].append(p3[..., P_SBV:P_GZ].reshape(bp, seq, SB_HEADS, SB_HEAD_DIM))
        outs[2].append(p3[:, seq - POOL_HIST:, P_POOL:P_SBQ])
        outs[3].append(p3[:, seq - (GDN_CONV - 1):, P_GQKV:P_POOL])
        outs[4].append(s_p)

        ps = rms_matmul(xs, norm1_g[l], w_in_p[l], tm=tm_s, tn=1024)
        o_pool_s, qn_s, kn_s, gv_s, gb_s = sample_pre(ps, state_pool[l], state_gdn_conv[l], pool_w_bd[l],
                                                      pool_scale[l], gdn_conv_w[l], alog_p[l:l + 1],
                                                      dtb_p[l:l + 1], pos0=past_len)
        o_sb_s = sb_decode(ps[:, P_SBQ:P_SBK], cache_kt, cache_vt, page_table, sb_bias[l], l, group=page_group)
        s_s, o_g_s = gdn_step(state_gdn[l], kn_s, qn_s, gv_s, ps[:, P_GZ:P_AB], gb_s[:, :GDN_HEADS],
                              gb_s[:, GDN_HEADS:2 * GDN_HEADS], gdn_norm_g[l], bt=bt)
        xs = merge(xs, o_pool_s, o_sb_s, o_g_s.reshape(bs, GDN_V_W), ps, w_branch_b[l], w_o_b[l], tm=tm_s)
        xs = ffn(xs, norm2_g[l], w_ffn_in_b[l], w_ffn_out_b[l], normf_g, tm=tm_s, tf=256, final=final)
        outs[5].append(ps[:, P_SBK:P_SBV].reshape(bs, 1, SB_HEADS, SB_HEAD_DIM))
        outs[6].append(ps[:, P_SBV:P_GZ].reshape(bs, 1, SB_HEADS, SB_HEAD_DIM))
        outs[7].append(jnp.concatenate([state_pool[l], ps[:, None, P_POOL:P_SBQ]], axis=1)[:, 1:])
        outs[8].append(jnp.concatenate([state_gdn_conv[l], ps[:, None, P_GQKV:P_POOL]], axis=1)[:, 1:])
        outs[9].append(s_s)

    y_prompt = xp.reshape(bp, seq, d)
    y_sample = xs.reshape(bs, 1, d)
    return (y_prompt, y_sample) + tuple(jnp.stack(o) for o in outs)
```

```python
import functools

import numpy as np
import jax
import jax.numpy as jnp
from jax import lax
from jax.experimental import pallas as pl
from jax.experimental.pallas import tpu as pltpu

F32 = jnp.float32
BF16 = jnp.bfloat16

D_MODEL = 1024
POOL_WINDOWS = (2, 4, 8, 16)
POOL_W = 256
POOL_GC = 64
POOL_HIST = 15
SB_HEADS = 8
SB_HEAD_DIM = 64
SB_W = 512
GDN_HEADS = 4
GDN_DK = 64
GDN_DV = 64
GDN_QK_W = 256
GDN_V_W = 256
GDN_QKV_W = 768
GDN_CONV = 4
GDN_CHUNK = 64
D_FF = 2816
RMS_EPS = 1e-6
L2_EPS = 1e-6

R_POOL, R_SBQ, R_GQKV, R_GA, R_GZ, R_GATE, R_END = 0, 256, 1792, 2560, 2568, 2824, 5896
P_GQKV, P_POOL, P_SBQ, P_SBK, P_SBV, P_GZ, P_AB, P_GATE, P_W = 0, 768, 1024, 1536, 2048, 2560, 2816, 3072, 6144

LANES = 128
SB_TK = LANES
VMEM_LIMIT = 56 * 1024 * 1024


def _cparams(sem):
    return pltpu.CompilerParams(dimension_semantics=sem, vmem_limit_bytes=VMEM_LIMIT)


def _sigmoid(x):
    return 1.0 / (1.0 + jnp.exp(-x))


def _softplus(x):
    return jnp.maximum(x, 0.0) + jnp.log(1.0 + jnp.exp(-jnp.abs(x)))


def _split(x):
    hi = x.astype(BF16)
    lo = (x - hi.astype(F32)).astype(BF16)
    return hi, lo


def _dot(a, b):
    return jnp.dot(a, b, preferred_element_type=F32)


def _dot_nt(a, b):
    return lax.dot_general(a, b, (((1,), (1,)), ((), ())), preferred_element_type=F32)


def _dot_tn(a, b):
    return lax.dot_general(a, b, (((0,), (0,)), ((), ())), preferred_element_type=F32)


def _mm3(a, b):
    ah, al = _split(a)
    bh, bl = _split(b)
    return _dot(ah, bh) + (_dot(ah, bl) + _dot(al, bh))


def _dot_exact_rhs(x, m):
    hi, lo = _split(x)
    return _dot(hi, m) + _dot(lo, m)


def _rms_scale(x):
    return lax.rsqrt(jnp.mean(x * x, axis=-1, keepdims=True) + RMS_EPS)


def _alternate(*stage_gens):
    results = [None] * len(stage_gens)
    live = list(range(len(stage_gens)))
    while live:
        for i in list(live):
            try:
                next(stage_gens[i])
            except StopIteration as done:
                results[i] = done.value
                live.remove(i)
    return results


def _resident(shape):
    return pl.BlockSpec(shape, lambda *_: (0,) * len(shape), pipeline_mode=pl.Buffered(1))


def _rms_matmul_kernel(x_ref, g_ref, w_ref, o_ref, *, tn):
    x = x_ref[...]
    h = (x * _rms_scale(x) * g_ref[...]).astype(BF16)
    for j in range(w_ref.shape[1] // tn):
        o_ref[:, j * tn:(j + 1) * tn] = _dot(h, w_ref[:, j * tn:(j + 1) * tn])


def rms_matmul(x, g, w, *, tm, tn):
    t, d = x.shape
    n = w.shape[1]
    return pl.pallas_call(
        functools.partial(_rms_matmul_kernel, tn=tn),
        grid=(t // tm,),
        in_specs=[pl.BlockSpec((tm, d), lambda i: (i, 0)), _resident((1, d)), _resident((d, n))],
        out_specs=pl.BlockSpec((tm, n), lambda i: (i, 0)),
        out_shape=jax.ShapeDtypeStruct((t, n), F32),
        compiler_params=_cparams(("parallel",)),
        name="in_proj",
    )(x, g.reshape(1, d), w)


def _ffn_stages(x_ref, g_ref, wi_ref, wo_ref, gf_ref, o_ref, act_ref, *, final, tf):
    x = x_ref[...]
    h = (x * _rms_scale(x) * g_ref[...]).astype(BF16)
    dff = wo_ref.shape[0]
    for f in range(dff // tf):
        gate = _dot(h, wi_ref[:, f * tf:(f + 1) * tf])
        up = _dot(h, wi_ref[:, dff + f * tf:dff + (f + 1) * tf])
        act_ref[:, f * tf:(f + 1) * tf] = ((gate * _sigmoid(gate)) * up).astype(BF16)
        yield
    y = x + _dot(act_ref[...], wo_ref[...])
    if final:
        y = y * _rms_scale(y) * gf_ref[...]
    o_ref[...] = y


def _ffn_kernel(*refs, final, tf):
    _alternate(_ffn_stages(*refs, final=final, tf=tf))


def _ffn_in_specs(tm, d, dff):
    return [pl.BlockSpec((tm, d), lambda i, *_: (i, 0)), _resident((1, d)), _resident((d, 2 * dff)),
            _resident((dff, d)), _resident((1, d))]


def ffn(x, g, w_in, w_out, gf, *, tm, tf, final):
    t, d = x.shape
    dff = w_out.shape[0]
    return pl.pallas_call(
        functools.partial(_ffn_kernel, final=final, tf=tf),
        grid=(t // tm,),
        in_specs=_ffn_in_specs(tm, d, dff),
        out_specs=pl.BlockSpec((tm, d), lambda i: (i, 0)),
        out_shape=jax.ShapeDtypeStruct((t, d), F32),
        scratch_shapes=[pltpu.VMEM((tm, dff), BF16)],
        compiler_params=_cparams(("parallel",)),
        name="ffn",
    )(x, g.reshape(1, d), w_in, w_out, gf.reshape(1, d))


def _merge_kernel(x_ref, op_ref, osb_ref, og_ref, g0_ref, g1_ref, g2_ref, wb_ref, wo_ref, o_ref):
    br_pool = _dot(op_ref[...].astype(BF16), wb_ref[0:POOL_W, :])
    br_sb = _dot(osb_ref[...].astype(BF16), wb_ref[POOL_W:POOL_W + SB_W, :])
    br_gdn = _dot(og_ref[...].astype(BF16), wb_ref[POOL_W + SB_W:, :])
    m = _sigmoid(g0_ref[...]) * br_pool + _sigmoid(g1_ref[...]) * br_sb + _sigmoid(g2_ref[...]) * br_gdn
    o_ref[...] = x_ref[...] + _dot(m.astype(BF16), wo_ref[...])


def merge(x, o_pool, o_sb, o_g, p, w_branch, w_o, *, tm):
    t, d = x.shape
    gblk = P_GATE // d
    return pl.pallas_call(
        _merge_kernel,
        grid=(t // tm,),
        in_specs=[pl.BlockSpec((tm, d), lambda i: (i, 0)),
                  pl.BlockSpec((tm, POOL_W), lambda i: (i, 0)),
                  pl.BlockSpec((tm, SB_W), lambda i: (i, 0)),
                  pl.BlockSpec((tm, GDN_V_W), lambda i: (i, 0)),
                  pl.BlockSpec((tm, d), lambda i: (i, gblk)),
                  pl.BlockSpec((tm, d), lambda i: (i, gblk + 1)),
                  pl.BlockSpec((tm, d), lambda i: (i, gblk + 2)),
                  _resident((d, d)), _resident((d, d))],
        out_specs=pl.BlockSpec((tm, d), lambda i: (i, 0)),
        out_shape=jax.ShapeDtypeStruct((t, d), F32),
        compiler_params=_cparams(("parallel",)),
        name="merge",
    )(x, o_pool, o_sb, o_g, p, p, p, w_branch, w_o)


def _pool_select(lane_grp, vals):
    out = vals[-1]
    for gi in range(len(vals) - 2, -1, -1):
        out = jnp.where(lane_grp == gi, vals[gi], out)
    return out


def _pool_prompt_kernel(u_ref, w_ref, sc_ref, o_ref):
    u = u_ref[0]
    t = lax.broadcasted_iota(jnp.int32, u.shape, 0)
    grp = lax.broadcasted_iota(jnp.int32, u.shape, 1) // POOL_GC

    def shifted(x, k):
        return jnp.where(t >= k, pltpu.roll(x, k, axis=0), 0.0)

    sums = []
    s, w = u, 1
    for win in POOL_WINDOWS:
        while w < win:
            s = s + shifted(s, w)
            w *= 2
        sums.append(s)
    sel = _pool_select(grp, sums)
    win = _pool_select(grp, [jnp.full(u.shape, wn, jnp.int32) for wn in POOL_WINDOWS])
    cnt = jnp.minimum(win, t + 1).astype(F32)
    pooled = sel / cnt - u
    o_ref[0] = _dot(pooled.astype(BF16), w_ref[...]) * sc_ref[...]


def pool_prompt(p3, w_bd, scale):
    b, l, _ = p3.shape
    return pl.pallas_call(
        _pool_prompt_kernel,
        grid=(b,),
        in_specs=[pl.BlockSpec((1, l, POOL_W), lambda i: (i, 0, P_POOL // POOL_W)),
                  pl.BlockSpec((POOL_W, POOL_W), lambda i: (0, 0)),
                  pl.BlockSpec((1, POOL_W), lambda i: (0, 0))],
        out_specs=pl.BlockSpec((1, l, POOL_W), lambda i: (i, 0, 0)),
        out_shape=jax.ShapeDtypeStruct((b, l, POOL_W), F32),
        compiler_params=_cparams(("parallel",)),
        name="pool_prompt",
    )(p3, w_bd, scale.reshape(1, POOL_W))


def _sb_log_terms(z, vis, suffix_mat):
    sp = _softplus(z)
    hi, lo = _split(sp if vis is None else jnp.where(vis, sp, 0.0))
    return sp, _dot(jnp.concatenate([hi, lo], axis=1), suffix_mat)


def _head_pair_rows(x, first_head_lanes):
    return jnp.concatenate([jnp.where(first_head_lanes, x, 0.0), jnp.where(first_head_lanes, 0.0, x)],
                           axis=0).astype(BF16)


def _sb_prompt_kernel(bias_ref, q_ref, k_ref, v_ref, u2_ref, o_ref, qb_ref, acc_ref, c_ref, *, tq):
    qi = pl.program_id(1)
    tk = SB_TK
    hw = 2 * SB_HEAD_DIM
    npair = SB_HEADS // 2
    n_diag = tq // tk
    head0 = lax.broadcasted_iota(jnp.int32, (tk, hw), 1) < SB_HEAD_DIM
    row = lax.broadcasted_iota(jnp.int32, (tq, tk), 0)
    col = lax.broadcasted_iota(jnp.int32, (tq, tk), 1)
    qb_ref[...] = (q_ref[0] * (SB_HEAD_DIM ** -0.5)).astype(BF16)
    acc_ref[...] = jnp.zeros_like(acc_ref)
    c_ref[...] = jnp.zeros_like(c_ref)
    suffix_mat = u2_ref[...]

    pairs = [slice(p * hw, (p + 1) * hw) for p in range(npair)]

    def key_blocks(blocks):
        z2 = [[_dot_nt(qb_ref[:, lanes], _head_pair_rows(k_ref[0, pl.ds(ks, tk), lanes], head0)) for lanes in pairs]
              for ks, _ in blocks]
        z = [[zb[h // 2][:, (h % 2) * tk:(h % 2 + 1) * tk] + bias_ref[h] for h in range(SB_HEADS)] for zb in z2]
        terms = [[_sb_log_terms(zh, vis, suffix_mat) for zh in zb] for zb, (_, vis) in zip(z, blocks)]
        a = [[None] * SB_HEADS for _ in blocks]
        for h in range(SB_HEADS):
            c = c_ref[h]
            for i, (_, vis) in enumerate(blocks):
                sp, s2 = terms[i][h]
                ah = jnp.exp(z[i][h] - sp - s2[:, :tk] - c)
                a[i][h] = (ah if vis is None else jnp.where(vis, ah, 0.0)).astype(BF16)
                c = c + s2[:, tk:]
            c_ref[h] = c
        for p, lanes in enumerate(pairs):
            upd = None
            for i, (ks, _) in enumerate(blocks):
                vv = _head_pair_rows(v_ref[0, pl.ds(ks, tk), lanes], head0)
                d = _dot(jnp.concatenate(a[i][2 * p:2 * p + 2], axis=1), vv)
                upd = d if upd is None else upd + d
            acc_ref[:, lanes] += upd

    key_blocks([(pl.multiple_of(qi * tq + d * tk, tk), (col + d * tk) < row) for d in range(n_diag - 1, -1, -1)])

    def full_body(jj, carry):
        last = qi * n_diag - 1 - jj * n_diag
        key_blocks([(pl.multiple_of((last - d) * tk, tk), None) for d in range(n_diag)])
        return carry

    lax.fori_loop(0, qi, full_body, 0)
    o_ref[0] = acc_ref[...]


def _suffix_matrix():
    j = np.arange(2 * SB_TK)[:, None] % SB_TK
    c = np.arange(2 * SB_TK)[None, :]
    return jnp.asarray(np.where(c < SB_TK, j > c, True), BF16)


def sb_prompt(p3, bias, *, tq):
    b, l, _ = p3.shape
    return pl.pallas_call(
        functools.partial(_sb_prompt_kernel, tq=tq),
        grid=(b, l // tq),
        in_specs=[pl.BlockSpec(memory_space=pltpu.SMEM),
                  pl.BlockSpec((1, tq, SB_W), lambda i, q: (i, q, P_SBQ // SB_W)),
                  pl.BlockSpec((1, l, SB_W), lambda i, q: (i, 0, P_SBK // SB_W)),
                  pl.BlockSpec((1, l, SB_W), lambda i, q: (i, 0, P_SBV // SB_W)),
                  pl.BlockSpec((2 * SB_TK, 2 * SB_TK), lambda i, q: (0, 0))],
        out_specs=pl.BlockSpec((1, tq, SB_W), lambda i, q: (i, q, 0)),
        out_shape=jax.ShapeDtypeStruct((b, l, SB_W), F32),
        scratch_shapes=[pltpu.VMEM((tq, SB_W), BF16), pltpu.VMEM((tq, SB_W), F32),
                        pltpu.VMEM((SB_HEADS, tq, SB_TK), F32)],
        compiler_params=_cparams(("parallel", "arbitrary")),
        name="sb_prompt",
    )(bias, p3, p3, p3, _suffix_matrix())


def _sb_own_head():
    return (lax.broadcasted_iota(jnp.int32, (SB_HEADS, SB_W), 0)
            == lax.broadcasted_iota(jnp.int32, (SB_HEADS, SB_W), 1) // SB_HEAD_DIM)


def _sb_decode_stages(q_ref, bias_ref, u2_ref, k_refs, v_refs, carry, acc, *, per_stage=4):
    nh = SB_HEADS
    group = len(k_refs)
    q = jnp.broadcast_to(q_ref[0] * (SB_HEAD_DIM ** -0.5), (nh, SB_W))
    qbd = jnp.where(_sb_own_head(), q, 0.0).astype(BF16)
    zs = []
    for g in range(group):
        zs.append(_dot(qbd, k_refs[g][...].astype(BF16)) + bias_ref[...])
        if g % per_stage == per_stage - 1:
            yield
    z = jnp.concatenate(zs, axis=0)
    sp, s2 = _sb_log_terms(z, None, u2_ref[...])
    yield
    later = [None] * group
    for g in range(group - 1, -1, -1):
        later[g] = carry
        carry = carry + s2[g * nh:(g + 1) * nh, SB_TK:]
    a = jnp.exp(z - sp - s2[:, :SB_TK] - jnp.concatenate(later, axis=0))
    yield
    for g in range(group):
        acc = acc + _dot_nt(a[g * nh:(g + 1) * nh].astype(BF16), v_refs[g][...].astype(BF16))
        if g % per_stage == per_stage - 1:
            yield
    return carry, acc


def _sb_decode_kernel(pt_ref, q_ref, bias_ref, u2_ref, *refs, group):
    k_refs, v_refs = refs[:group], refs[group:2 * group]
    o_ref, c_ref, acc_ref = refs[2 * group:]
    step = pl.program_id(1)

    @pl.when(step == 0)
    def _():
        c_ref[...] = jnp.zeros_like(c_ref)
        acc_ref[...] = jnp.zeros_like(acc_ref)

    (carry, acc), = _alternate(_sb_decode_stages(q_ref, bias_ref, u2_ref, k_refs, v_refs, c_ref[...], acc_ref[...]))
    c_ref[...] = carry
    acc_ref[...] = acc

    @pl.when(step == pl.num_programs(1) - 1)
    def _():
        o_ref[0] = jnp.sum(jnp.where(_sb_own_head(), acc, 0.0), axis=0, keepdims=True)


def sb_decode(q, cache_kt, cache_vt, page_table, bias, layer, *, group):
    bs, n_pages = page_table.shape
    page = cache_kt.shape[3]
    assert page == SB_TK and n_pages % group == 0
    n_steps = n_pages // group

    def page_spec(g):
        def index(b, s, pt):
            return (layer, pt[b * n_pages + (n_steps - 1 - s) * group + g], 0, 0)
        return pl.BlockSpec((None, None, SB_W, page), index)

    grid_spec = pltpu.PrefetchScalarGridSpec(
        num_scalar_prefetch=1,
        grid=(bs, n_steps),
        in_specs=[pl.BlockSpec((1, 1, SB_W), lambda b, s, pt: (b, 0, 0)),
                  pl.BlockSpec((SB_HEADS, 1), lambda b, s, pt: (0, 0)),
                  pl.BlockSpec((2 * SB_TK, 2 * SB_TK), lambda b, s, pt: (0, 0))]
        + [page_spec(g) for g in range(group)] * 2,
        out_specs=pl.BlockSpec((1, 1, SB_W), lambda b, s, pt: (b, 0, 0)),
        scratch_shapes=[pltpu.VMEM((SB_HEADS, SB_TK), F32), pltpu.VMEM((SB_HEADS, SB_W), F32)],
    )
    out = pl.pallas_call(
        functools.partial(_sb_decode_kernel, group=group),
        grid_spec=grid_spec,
        out_shape=jax.ShapeDtypeStruct((bs, 1, SB_W), F32),
        compiler_params=_cparams(("parallel", "arbitrary")),
        name="sb_decode",
    )(page_table.reshape(-1), q.reshape(bs, 1, SB_W), bias.reshape(SB_HEADS, 1), _suffix_matrix(),
      *([cache_kt] * group), *([cache_vt] * group))
    return out.reshape(bs, SB_W)


def _ffn_decode_kernel(pt_ref, x_ref, g_ref, wi_ref, wo_ref, gf_ref, q_ref, bias_ref, u2_ref, *refs, n_pages, final, tf):
    k_refs, v_refs = refs[:n_pages], refs[n_pages:2 * n_pages]
    o_ref, od_ref, act_ref = refs[2 * n_pages:]
    zero = lambda w: jnp.zeros((SB_HEADS, w), F32)
    _, (_, acc) = _alternate(
        _ffn_stages(x_ref, g_ref, wi_ref, wo_ref, gf_ref, o_ref, act_ref, final=final, tf=tf),
        _sb_decode_stages(q_ref, bias_ref, u2_ref, k_refs, v_refs, zero(SB_TK), zero(SB_W), per_stage=2))
    od_ref[0] = jnp.sum(jnp.where(_sb_own_head(), acc, 0.0), axis=0, keepdims=True)


def ffn_decode(x, g, w_in, w_out, gf, q, cache_kt, cache_vt, page_table, bias, layer, *, tf, final):
    t, d = x.shape
    dff = w_out.shape[0]
    bs, n_pages = page_table.shape
    page = cache_kt.shape[3]
    tm = t // bs
    assert page == SB_TK and tm * bs == t and tm % 8 == 0

    def page_spec(g):
        return pl.BlockSpec((None, None, SB_W, page), lambda i, pt: (layer, pt[i * n_pages + g], 0, 0))

    grid_spec = pltpu.PrefetchScalarGridSpec(
        num_scalar_prefetch=1,
        grid=(bs,),
        in_specs=_ffn_in_specs(tm, d, dff)
        + [pl.BlockSpec((1, 1, SB_W), lambda i, pt: (i, 0, 0)), _resident((SB_HEADS, 1)),
           _resident((2 * SB_TK, 2 * SB_TK))]
        + [page_spec(g) for g in range(n_pages)] * 2,
        out_specs=[pl.BlockSpec((tm, d), lambda i, pt: (i, 0)), pl.BlockSpec((1, 1, SB_W), lambda i, pt: (i, 0, 0))],
        scratch_shapes=[pltpu.VMEM((tm, dff), BF16)],
    )
    y, o = pl.pallas_call(
        functools.partial(_ffn_decode_kernel, n_pages=n_pages, final=final, tf=tf),
        grid_spec=grid_spec,
        out_shape=[jax.ShapeDtypeStruct((t, d), F32), jax.ShapeDtypeStruct((bs, 1, SB_W), F32)],
        compiler_params=_cparams(("parallel",)),
        name="ffn_decode",
    )(page_table.reshape(-1), x, g.reshape(1, d), w_in, w_out, gf.reshape(1, d), q.reshape(bs, 1, SB_W),
      bias.reshape(SB_HEADS, 1), _suffix_matrix(), *([cache_kt] * n_pages), *([cache_vt] * n_pages))
    return y, o.reshape(bs, SB_W)


def _head_indicator(width, head):
    i = np.arange(width) // head
    return jnp.asarray(i[:, None] == i[None, :], BF16)


def _gdn_qkv_post(conv_out, seg):
    act = conv_out * _sigmoid(conv_out)
    q = act[:, :GDN_QK_W]
    k = act[:, GDN_QK_W:2 * GDN_QK_W]
    v = act[:, 2 * GDN_QK_W:]
    qn = q * lax.rsqrt(_dot_exact_rhs(q * q, seg) + L2_EPS) * (GDN_DK ** -0.5)
    kn = k * lax.rsqrt(_dot_exact_rhs(k * k, seg) + L2_EPS)
    return qn, kn, v


def _gdn_gates(ab, alog, dtb):
    g = -jnp.exp(alog) * _softplus(ab + dtb)
    return g, _sigmoid(ab)


def _gdn_pre_kernel(x_ref, halo_ref, ab_ref, cw_ref, alog_ref, dtb_ref, seg_ref, q_ref, k_ref, v_ref, gb_ref):
    i = pl.program_id(1)
    x = x_ref[0]
    tl = x.shape[0]
    halo = jnp.where(i > 0, halo_ref[0], 0.0)
    nh = halo.shape[0]
    ext = jnp.concatenate([halo, x], axis=0)
    out = None
    for tap in range(GDN_CONV):
        d = GDN_CONV - 1 - tap
        src = x if d == 0 else pltpu.roll(ext, d, axis=0)[nh:]
        term = src * cw_ref[tap:tap + 1, :]
        out = term if out is None else out + term
    qn, kn, v = _gdn_qkv_post(out, seg_ref[...])
    q_ref[0] = qn
    k_ref[0] = kn
    v_ref[0] = v

    g, beta = _gdn_gates(ab_ref[0], alog_ref[...], dtb_ref[...])
    tt = lax.broadcasted_iota(jnp.int32, g.shape, 0) % GDN_CHUNK
    k = 1
    while k < GDN_CHUNK:
        g = g + jnp.where(tt >= k, pltpu.roll(g, k, axis=0), 0.0)
        k *= 2
    lane = lax.broadcasted_iota(jnp.int32, g.shape, 1)
    gb_ref[0] = jnp.where(lane < GDN_HEADS, g, beta)


def gdn_pre(p3, conv_w, alog, dtb, *, tl):
    b, l, _ = p3.shape
    nh = 8
    out_sd = jax.ShapeDtypeStruct((b, l, GDN_QK_W), F32)
    return pl.pallas_call(
        _gdn_pre_kernel,
        grid=(b, l // tl),
        in_specs=[pl.BlockSpec((1, tl, GDN_QKV_W), lambda n, i: (n, i, P_GQKV // GDN_QKV_W)),
                  pl.BlockSpec((1, nh, GDN_QKV_W), lambda n, i: (n, jnp.maximum(i * (tl // nh) - 1, 0), 0)),
                  pl.BlockSpec((1, tl, LANES), lambda n, i: (n, i, P_AB // LANES)),
                  pl.BlockSpec((GDN_CONV, GDN_QKV_W), lambda n, i: (0, 0)),
                  pl.BlockSpec((1, LANES), lambda n, i: (0, 0)),
                  pl.BlockSpec((1, LANES), lambda n, i: (0, 0)),
                  pl.BlockSpec((GDN_QK_W, GDN_QK_W), lambda n, i: (0, 0))],
        out_specs=[pl.BlockSpec((1, tl, GDN_QK_W), lambda n, i: (n, i, 0))] * 3
        + [pl.BlockSpec((1, tl, LANES), lambda n, i: (n, i, 0))],
        out_shape=[out_sd, out_sd, out_sd, jax.ShapeDtypeStruct((b, l, LANES), F32)],
        compiler_params=_cparams(("parallel", "parallel")),
        name="gdn_pre",
    )(p3, p3, p3, conv_w, alog, dtb, _head_indicator(GDN_QK_W, GDN_DK))


def _gate_broadcast_matrix():
    r = np.arange(2 * LANES)[:, None] % LANES
    c = np.arange(2 * LANES)[None, :]
    return jnp.asarray(np.where(c < LANES, r < GDN_HEADS, (r >= GDN_HEADS) & (r < 2 * GDN_HEADS)), BF16)


def _gdn_chunk_terms(q, k, v, gb, e_mat):
    ch = GDN_CHUNK
    r = GDN_HEADS * ch
    sh = ch.bit_length() - 1
    chunks = range(len(q))
    ri = lax.broadcasted_iota(jnp.int32, (r, r), 0)
    ci = lax.broadcasted_iota(jnp.int32, (r, r), 1)
    same_head = (ri >> sh) == (ci >> sh)
    incl = same_head & (ri >= ci)
    strict = same_head & (ri > ci)
    eye = (ri == ci).astype(F32)

    def stack(x):
        return jnp.where(same_head, jnp.concatenate([x] * GDN_HEADS, axis=0), 0.0)

    def widen(x):
        return jnp.concatenate([x] * (r // LANES), axis=1)

    kbd, qbd, vbd = [stack(x) for x in k], [stack(x) for x in q], [stack(x) for x in v]
    yield

    l1 = lax.broadcasted_iota(jnp.int32, (r, LANES), 1)
    r1 = lax.broadcasted_iota(jnp.int32, (r, LANES), 0) >> sh
    gate_lane = (l1 == r1) | (l1 == r1 + GDN_HEADS)
    gsel = [_split(jnp.where(gate_lane, jnp.concatenate([x] * GDN_HEADS, axis=0), 0.0)) for x in gb]
    gbc = [_dot(jnp.concatenate([hi, lo], axis=1), e_mat) for hi, lo in gsel]
    g1 = [x[:, :LANES] for x in gbc]
    gl1 = [jnp.concatenate([jnp.broadcast_to(x[(h + 1) * ch - 1:(h + 1) * ch, :], (ch, LANES))
                            for h in range(GDN_HEADS)], axis=0) for x in g1]
    gcol = [widen(x) for x in g1]
    bcol = [widen(x[:, LANES:]) for x in gbc]
    eg = [widen(jnp.exp(x)) for x in g1]
    egl = [widen(jnp.exp(x)) for x in gl1]
    ekd = [widen(jnp.exp(gl1[i] - g1[i])) for i in chunks]

    yield
    decay = [jnp.where(incl, jnp.exp(jnp.where(incl, x - x.T, 0.0)), 0.0) for x in gcol]
    kb = [kbd[i] * bcol[i] for i in chunks]
    kbd_b = [x.astype(BF16) for x in kbd]
    a_mat = [jnp.where(strict, _dot_nt(kb[i].astype(BF16), kbd_b[i]) * decay[i], 0.0) for i in chunks]
    yield

    t_inv = [eye - jnp.where((ri >> 1) == (ci >> 1), x, 0.0) for x in a_mat]
    for lv in range(1, sh):
        off = ((ri >> (lv + 1)) == (ci >> (lv + 1))) & ((ri >> lv) != (ci >> lv))
        tb = [x.astype(BF16) for x in t_inv]
        m = [_dot(jnp.where(off, a_mat[i], 0.0).astype(BF16), tb[i]).astype(BF16) for i in chunks]
        yield
        t_inv = [t_inv[i] - _dot(tb[i], m[i]) for i in chunks]
        yield
    res = [(eye - t_inv[i]) - _mm3(a_mat[i], t_inv[i]) for i in chunks]
    yield
    t_inv = [t_inv[i] + _dot(t_inv[i].astype(BF16), res[i].astype(BF16)) for i in chunks]
    t_split = [_split(x) for x in t_inv]
    yield

    def apply_t(i, x):
        th, tl = t_split[i]
        xh, xl = _split(x)
        return _dot(th, xh) + (_dot(th, xl) + _dot(tl, xh))

    u = [apply_t(i, vbd[i] * bcol[i]) for i in chunks]
    yield
    w = [apply_t(i, kb[i] * eg[i]).astype(BF16) for i in chunks]
    yield
    qk = [jnp.where(incl, _dot_nt(qbd[i].astype(BF16), kbd_b[i]) * decay[i], 0.0).astype(BF16) for i in chunks]
    qd = [(qbd[i] * eg[i]).astype(BF16) for i in chunks]
    kd = [(kbd[i] * ekd[i]).astype(BF16) for i in chunks]
    return list(zip(u, w, qk, qd, kd, egl))


def _gdn_chunk_kernel(q_ref, k_ref, v_ref, gb_ref, z_ref, gn_ref, e_ref, seg_ref, og_ref, s_ref, st_ref, *, nc):
    c = pl.program_id(1)
    ch = GDN_CHUNK

    @pl.when(c == 0)
    def _():
        st_ref[...] = jnp.zeros_like(st_ref)

    def terms_of(chunk_ids):
        rows = [slice(i * ch, (i + 1) * ch) for i in chunk_ids]
        return _gdn_chunk_terms([q_ref[0, x, :] for x in rows], [k_ref[0, x, :] for x in rows],
                                [v_ref[0, x, :] for x in rows], [gb_ref[0, x, :] for x in rows], e_ref[...])

    state = {"s": st_ref[...]}
    outs = []

    def recurrence(terms):
        for u, w, qk, qd, kd, egl in terms:
            sb = state["s"].astype(BF16)
            vnb = (u - _dot(w, sb)).astype(BF16)
            o = _dot(qd, sb)
            yield
            o = o + _dot(qk, vnb)
            state["s"] = state["s"] * egl + _dot_tn(kd, vnb)
            oc = o[0:ch]
            for h in range(1, GDN_HEADS):
                oc = oc + o[h * ch:(h + 1) * ch]
            outs.append(oc)
            yield

    half = max(nc // 2, 1)
    terms_a, = _alternate(terms_of(range(half)))
    if nc > half:
        terms_b, _ = _alternate(terms_of(range(half, nc)), recurrence(terms_a))
        _alternate(recurrence(terms_b))
    else:
        _alternate(recurrence(terms_a))
    s = state["s"]
    st_ref[...] = s

    oc = jnp.concatenate(outs, axis=0)
    ms = _dot_exact_rhs(oc * oc, seg_ref[...]) * (1.0 / GDN_DV)
    zz = z_ref[0]
    og_ref[0] = oc * lax.rsqrt(ms + RMS_EPS) * gn_ref[...] * (zz * _sigmoid(zz))

    @pl.when(c == pl.num_programs(1) - 1)
    def _():
        for h in range(GDN_HEADS):
            s_ref[0, h] = s[h * GDN_DK:(h + 1) * GDN_DK, h * GDN_DV:(h + 1) * GDN_DV]


def gdn_chunk(qn, kn, v, gb, p3, gnorm, *, nc):
    b, l, _ = qn.shape
    ch = GDN_CHUNK * nc
    r = GDN_HEADS * GDN_CHUNK
    blk = lambda w: pl.BlockSpec((1, ch, w), lambda n, c: (n, c, 0))
    return pl.pallas_call(
        functools.partial(_gdn_chunk_kernel, nc=nc),
        grid=(b, l // ch),
        in_specs=[blk(GDN_QK_W), blk(GDN_QK_W), blk(GDN_V_W), blk(LANES),
                  pl.BlockSpec((1, ch, GDN_V_W), lambda n, c: (n, c, P_GZ // GDN_V_W)),
                  pl.BlockSpec((1, GDN_V_W), lambda n, c: (0, 0)),
                  pl.BlockSpec((2 * LANES, 2 * LANES), lambda n, c: (0, 0)),
                  pl.BlockSpec((GDN_V_W, GDN_V_W), lambda n, c: (0, 0))],
        out_specs=[pl.BlockSpec((1, ch, GDN_V_W), lambda n, c: (n, c, 0)),
                   pl.BlockSpec((1, GDN_HEADS, GDN_DK, GDN_DV), lambda n, c: (n, 0, 0, 0))],
        out_shape=[jax.ShapeDtypeStruct((b, l, GDN_V_W), F32),
                   jax.ShapeDtypeStruct((b, GDN_HEADS, GDN_DK, GDN_DV), F32)],
        scratch_shapes=[pltpu.VMEM((r, r), F32)],
        compiler_params=_cparams(("parallel", "arbitrary")),
        name="gdn_chunk",
    )(qn, kn, v, gb, p3, jnp.tile(gnorm, GDN_HEADS).reshape(1, GDN_V_W), _gate_broadcast_matrix(),
      _head_indicator(GDN_V_W, GDN_DV))


def _sample_pre_kernel(up_ref, x_ref, ab_ref, ph_ref, ch_ref, pw_ref, psc_ref, cw_ref, alog_ref, dtb_ref, seg_ref,
                       op_ref, q_ref, k_ref, v_ref, gb_ref, *, pos0):
    u = up_ref[...]
    ph = ph_ref[...]
    hist = ph.shape[1]
    ti = lax.broadcasted_iota(jnp.int32, ph.shape, 1)
    grp = lax.broadcasted_iota(jnp.int32, u.shape, 1) // POOL_GC
    means = []
    for win in POOL_WINDOWS:
        n_hist = min(win - 1, hist)
        tail = jnp.sum(jnp.where(ti >= hist - n_hist, ph, 0.0), axis=1)
        means.append((tail + u) / float(min(win, pos0 + 1)))
    pooled = _pool_select(grp, means) - u
    op_ref[...] = _dot(pooled.astype(BF16), pw_ref[...]) * psc_ref[...]

    cw = cw_ref[...]
    out = jnp.sum(ch_ref[...] * cw[None, :GDN_CONV - 1, :], axis=1) + x_ref[...] * cw[GDN_CONV - 1:, :]
    qn, kn, v = _gdn_qkv_post(out, seg_ref[...])
    q_ref[...] = qn
    k_ref[...] = kn
    v_ref[...] = v

    g, beta = _gdn_gates(ab_ref[...], alog_ref[...], dtb_ref[...])
    lane = lax.broadcasted_iota(jnp.int32, g.shape, 1)
    gb_ref[...] = jnp.where(lane < GDN_HEADS, jnp.exp(g), beta)


def sample_pre(p, pool_hist, conv_hist, pool_w_bd, pool_scale, conv_w, alog, dtb, *, pos0):
    bs = p.shape[0]
    full = lambda shape: pl.BlockSpec(shape, lambda i: (0,) * len(shape))
    sd = lambda w: jax.ShapeDtypeStruct((bs, w), F32)
    return pl.pallas_call(
        functools.partial(_sample_pre_kernel, pos0=pos0),
        grid=(1,),
        in_specs=[pl.BlockSpec((bs, POOL_W), lambda i: (0, P_POOL // POOL_W)),
                  pl.BlockSpec((bs, GDN_QKV_W), lambda i: (0, P_GQKV // GDN_QKV_W)),
                  pl.BlockSpec((bs, LANES), lambda i: (0, P_AB // LANES)),
                  full(pool_hist.shape), full(conv_hist.shape), full((POOL_W, POOL_W)), full((1, POOL_W)),
                  full((GDN_CONV, GDN_QKV_W)), full((1, LANES)), full((1, LANES)), full((GDN_QK_W, GDN_QK_W))],
        out_specs=[full((bs, POOL_W)), full((bs, GDN_QK_W)), full((bs, GDN_QK_W)), full((bs, GDN_V_W)),
                   full((bs, LANES))],
        out_shape=[sd(POOL_W), sd(GDN_QK_W), sd(GDN_QK_W), sd(GDN_V_W), sd(LANES)],
        compiler_params=_cparams(("arbitrary",)),
        name="sample_pre",
    )(p, p, p, pool_hist, conv_hist, pool_w_bd, pool_scale.reshape(1, POOL_W), conv_w, alog, dtb,
      _head_indicator(GDN_QK_W, GDN_DK))


def _gdn_step_kernel(s_ref, k_ref, q_ref, v_ref, z_ref, eg_ref, beta_ref, gn_ref, so_ref, og_ref):
    k = k_ref[...]
    s = s_ref[...] * eg_ref[...]
    delta = (v_ref[...] - jnp.sum(k * s, axis=2, keepdims=True)) * beta_ref[...]
    s = s + k * delta
    so_ref[...] = s
    o = jnp.sum(q_ref[...] * s, axis=2, keepdims=True)
    zz = z_ref[...]
    og_ref[...] = o * _rms_scale(o) * gn_ref[...] * (zz * _sigmoid(zz))


def gdn_step(state, kn, qn, v, z, eg, beta, gnorm, *, bt):
    bs = state.shape[0]
    h, dk, dv = GDN_HEADS, GDN_DK, GDN_DV
    blk = lambda a, c: pl.BlockSpec((bt, h, a, c), lambda i: (i, 0, 0, 0))
    return pl.pallas_call(
        _gdn_step_kernel,
        grid=(bs // bt,),
        in_specs=[blk(dk, dv), blk(dk, 1), blk(dk, 1), blk(1, dv), blk(1, dv), blk(1, 1), blk(1, 1),
                  pl.BlockSpec((1, 1, 1, dv), lambda i: (0, 0, 0, 0))],
        out_specs=[blk(dk, dv), blk(1, dv)],
        out_shape=[jax.ShapeDtypeStruct((bs, h, dk, dv), F32), jax.ShapeDtypeStruct((bs, h, 1, dv), F32)],
        compiler_params=_cparams(("parallel",)),
        name="gdn_step",
    )(state, kn.reshape(bs, h, dk, 1), qn.reshape(bs, h, dk, 1), v.reshape(bs, h, 1, dv), z.reshape(bs, h, 1, dv),
      eg.reshape(bs, h, 1, 1), beta.reshape(bs, h, 1, 1), gnorm.reshape(1, 1, 1, dv))


def _largest_tile(n, cap):
    t = min(n, cap)
    while n % t:
        t //= 2
    return t


def _pack_w_in(w_in):
    depth, d, _ = w_in.shape
    pad = jnp.zeros((depth, d, P_GATE - P_AB - (R_GZ - R_GA)), w_in.dtype)
    return jnp.concatenate([w_in[..., R_GQKV:R_GA], w_in[..., R_POOL:R_GQKV], w_in[..., R_GZ:R_GATE],
                            w_in[..., R_GA:R_GZ], pad, w_in[..., R_GATE:R_END]], axis=-1).astype(BF16)


def _lane_pad(x):
    return jnp.pad(x, ((0, 0), (0, LANES - x.shape[1])))


def kernel(x_prompt, x_sample, cache_sb_k, cache_sb_v, page_table, state_pool, state_gdn_conv, state_gdn, norm1_g, w_in, pool_w, pool_scale, sb_bias, gdn_conv_w, gdn_a_log, gdn_dt_bias, gdn_norm_g, w_branch, w_o, norm2_g, w_ffn_in, w_ffn_out, normf_g):
    bp, seq, d = x_prompt.shape
    bs = x_sample.shape[0]
    depth = w_in.shape[0]
    n_phys, page = cache_sb_k.shape[1], cache_sb_k.shape[2]
    past_len = page_table.shape[1] * page
    tp = bp * seq

    w_in_p = _pack_w_in(w_in)
    w_branch_b = w_branch.astype(BF16)
    w_o_b = w_o.astype(BF16)
    w_ffn_in_b = w_ffn_in.astype(BF16)
    w_ffn_out_b = w_ffn_out.astype(BF16)
    eye_g = jnp.eye(len(POOL_WINDOWS), dtype=F32)
    pool_w_bd = jnp.einsum("lgcd,gh->lgchd", pool_w, eye_g).reshape(depth, POOL_W, POOL_W).astype(BF16)
    alog_p = _lane_pad(gdn_a_log)
    dtb_p = _lane_pad(gdn_dt_bias)
    cache_kt = jnp.transpose(cache_sb_k, (0, 1, 3, 4, 2)).reshape(depth, n_phys, SB_W, page)
    cache_vt = jnp.transpose(cache_sb_v, (0, 1, 3, 4, 2)).reshape(depth, n_phys, SB_W, page)
    n_pages = page_table.shape[1]
    page_group = _largest_tile(n_pages, 16)
    fuse_decode = page_group == n_pages and tp % bs == 0 and (tp // bs) % 8 == 0

    tm_p = _largest_tile(tp, 512)
    tm_merge = _largest_tile(tp, 512)
    tm_s = _largest_tile(bs, 128)
    tq = _largest_tile(seq, 256)
    tl = _largest_tile(seq, 256)
    bt = _largest_tile(bs, 8)

    xp = x_prompt.reshape(tp, d)
    xs = x_sample.reshape(bs, d)
    outs = [[] for _ in range(10)]
    for l in range(depth):
        final = l == depth - 1
        p = rms_matmul(xp, norm1_g[l], w_in_p[l], tm=tm_p, tn=1024)
        p3 = p.reshape(bp, seq, P_W)
        o_pool = pool_prompt(p3, pool_w_bd[l], pool_scale[l])
        o_sb = sb_prompt(p3, sb_bias[l], tq=tq)
        qn, kn, gv, gb = gdn_pre(p3, gdn_conv_w[l], alog_p[l:l + 1], dtb_p[l:l + 1], tl=tl)
        o_g, s_p = gdn_chunk(qn, kn, gv, gb, p3, gdn_norm_g[l], nc=_largest_tile(seq // GDN_CHUNK, 8))
        xp = merge(xp, o_pool.reshape(tp, POOL_W), o_sb.reshape(tp, SB_W), o_g.reshape(tp, GDN_V_W), p,
                   w_branch_b[l], w_o_b[l], tm=tm_merge)
        ps = rms_matmul(xs, norm1_g[l], w_in_p[l], tm=tm_s, tn=1024)
        if fuse_decode:
            xp, o_sb_s = ffn_decode(xp, norm2_g[l], w_ffn_in_b[l], w_ffn_out_b[l], normf_g, ps[:, P_SBQ:P_SBK],
                                    cache_kt, cache_vt, page_table, sb_bias[l], l, tf=256, final=final)
        else:
            xp = ffn(xp, norm2_g[l], w_ffn_in_b[l], w_ffn_out_b[l], normf_g, tm=tm_p, tf=256, final=final)
            o_sb_s = sb_decode(ps[:, P_SBQ:P_SBK], cache_kt, cache_vt, page_table, sb_bias[l], l, group=page_group)
        outs[0].append(p3[..., P_SBK:P_SBV].reshape(bp, seq, SB_HEADS, SB_HEAD_DIM))
        outs[1].append(p3[..., P_SBV:P_GZ].reshape(bp, seq, SB_HEADS, SB_HEAD_DIM))
        outs[2].append(p3[:, seq - POOL_HIST:, P_POOL:P_SBQ])
        outs[3].append(p3[:, seq - (GDN_CONV - 1):, P_GQKV:P_POOL])
        outs[4].append(s_p)

        o_pool_s, qn_s, kn_s, gv_s, gb_s = sample_pre(ps, state_pool[l], state_gdn_conv[l], pool_w_bd[l],
                                                      pool_scale[l], gdn_conv_w[l], alog_p[l:l + 1],
                                                      dtb_p[l:l + 1], pos0=past_len)
        s_s, o_g_s = gdn_step(state_gdn[l], kn_s, qn_s, gv_s, ps[:, P_GZ:P_AB], gb_s[:, :GDN_HEADS],
                              gb_s[:, GDN_HEADS:2 * GDN_HEADS], gdn_norm_g[l], bt=bt)
        xs = merge(xs, o_pool_s, o_sb_s, o_g_s.reshape(bs, GDN_V_W), ps, w_branch_b[l], w_o_b[l], tm=tm_s)
        xs = ffn(xs, norm2_g[l], w_ffn_in_b[l], w_ffn_out_b[l], normf_g, tm=tm_s, tf=256, final=final)
        outs[5].append(ps[:, P_SBK:P_SBV].reshape(bs, 1, SB_HEADS, SB_HEAD_DIM))
        outs[6].append(ps[:, P_SBV:P_GZ].reshape(bs, 1, SB_HEADS, SB_HEAD_DIM))
        outs[7].append(jnp.concatenate([state_pool[l], ps[:, None, P_POOL:P_SBQ]], axis=1)[:, 1:])
        outs[8].append(jnp.concatenate([state_gdn_conv[l], ps[:, None, P_GQKV:P_POOL]], axis=1)[:, 1:])
        outs[9].append(s_s)

    y_prompt = xp.reshape(bp, seq, d)
    y_sample = xs.reshape(bs, 1, d)
    return (y_prompt, y_sample) + tuple(jnp.stack(o) for o in outs)
```

```python
import functools

import numpy as np
import jax
import jax.numpy as jnp
from jax import lax
from jax.experimental import pallas as pl
from jax.experimental.pallas import tpu as pltpu

F32 = jnp.float32
BF16 = jnp.bfloat16

D_MODEL = 1024
POOL_WINDOWS = (2, 4, 8, 16)
POOL_W = 256
POOL_GC = 64
POOL_HIST = 15
SB_HEADS = 8
SB_HEAD_DIM = 64
SB_W = 512
GDN_HEADS = 4
GDN_DK = 64
GDN_DV = 64
GDN_QK_W = 256
GDN_V_W = 256
GDN_QKV_W = 768
GDN_CONV = 4
GDN_CHUNK = 64
D_FF = 2816
RMS_EPS = 1e-6
L2_EPS = 1e-6

R_POOL, R_SBQ, R_GQKV, R_GA, R_GZ, R_GATE, R_END = 0, 256, 1792, 2560, 2568, 2824, 5896
P_GQKV, P_POOL, P_SBQ, P_SBK, P_SBV, P_GZ, P_AB, P_W = 0, 768, 1024, 1536, 2048, 2560, 2816, 3072

LANES = 128
SB_TK = LANES
VMEM_LIMIT = 56 * 1024 * 1024


def _cparams(sem):
    return pltpu.CompilerParams(dimension_semantics=sem, vmem_limit_bytes=VMEM_LIMIT)


def _sigmoid(x):
    return 1.0 / (1.0 + jnp.exp(-x))


def _softplus(x):
    return jnp.maximum(x, 0.0) + jnp.log(1.0 + jnp.exp(-jnp.abs(x)))


def _split(x):
    hi = x.astype(BF16)
    lo = (x - hi.astype(F32)).astype(BF16)
    return hi, lo


def _dot(a, b):
    return jnp.dot(a, b, preferred_element_type=F32)


def _dot_nt(a, b):
    return lax.dot_general(a, b, (((1,), (1,)), ((), ())), preferred_element_type=F32)


def _dot_tn(a, b):
    return lax.dot_general(a, b, (((0,), (0,)), ((), ())), preferred_element_type=F32)


def _mm3(a, b):
    ah, al = _split(a)
    bh, bl = _split(b)
    return _dot(ah, bh) + (_dot(ah, bl) + _dot(al, bh))


def _dot_exact_rhs(x, m):
    hi, lo = _split(x)
    return _dot(hi, m) + _dot(lo, m)


def _rms_scale(x):
    return lax.rsqrt(jnp.mean(x * x, axis=-1, keepdims=True) + RMS_EPS)


def _alternate(*stage_gens):
    results = [None] * len(stage_gens)
    live = list(range(len(stage_gens)))
    while live:
        for i in list(live):
            try:
                next(stage_gens[i])
            except StopIteration as done:
                results[i] = done.value
                live.remove(i)
    return results


def _resident(shape):
    return pl.BlockSpec(shape, lambda *_: (0,) * len(shape), pipeline_mode=pl.Buffered(1))


def _rms_matmul_kernel(x_ref, g_ref, w_ref, *refs, tn, kv_t):
    o_ref = refs[-3] if kv_t else refs[-1]
    x = x_ref[...]
    h = (x * _rms_scale(x) * g_ref[...]).astype(BF16)
    for j in range(w_ref.shape[1] // tn):
        o_ref[:, j * tn:(j + 1) * tn] = _dot(h, w_ref[:, j * tn:(j + 1) * tn])
    if kv_t:
        kt_ref, vt_ref = refs[-2:]
        kt_ref[0] = o_ref[:, P_SBK:P_SBV].T
        vt_ref[0] = o_ref[:, P_SBV:P_GZ].T


def rms_matmul(x, g, w, *, tm, tn, kv_t=None):
    t, d = x.shape
    n = w.shape[1]
    in_specs = [pl.BlockSpec((tm, d), lambda i: (i, 0)), _resident((1, d)), _resident((d, n))]
    out_specs = [pl.BlockSpec((tm, n), lambda i: (i, 0))]
    out_shape = [jax.ShapeDtypeStruct((t, n), F32)]
    operands = [x, g.reshape(1, d), w]
    aliases = {}
    if kv_t:
        layer, depth, seq, k_buf, v_buf = kv_t
        tiles = seq // tm
        t_spec = pl.BlockSpec((None, 1, SB_W, tm), lambda i: (layer, i // tiles, 0, i % tiles))
        out_specs += [t_spec, t_spec]
        out_shape += [jax.ShapeDtypeStruct((depth, t // seq, SB_W, seq), F32)] * 2
        if k_buf is not None:
            in_specs += [pl.BlockSpec(memory_space=pl.ANY)] * 2
            operands += [k_buf, v_buf]
            aliases = {3: 1, 4: 2}
    out = pl.pallas_call(
        functools.partial(_rms_matmul_kernel, tn=tn, kv_t=bool(kv_t)),
        grid=(t // tm,),
        in_specs=in_specs,
        out_specs=out_specs,
        out_shape=out_shape,
        input_output_aliases=aliases,
        compiler_params=_cparams(("parallel",)),
        name="in_proj",
    )(*operands)
    return out if kv_t else out[0]


def _ffn_stages(x_ref, g_ref, wi_ref, wo_ref, gf_ref, o_ref, act_ref, *, final, tf):
    x = x_ref[...]
    h = (x * _rms_scale(x) * g_ref[...]).astype(BF16)
    dff = wo_ref.shape[0]
    for f in range(dff // tf):
        gate = _dot(h, wi_ref[:, f * tf:(f + 1) * tf])
        up = _dot(h, wi_ref[:, dff + f * tf:dff + (f + 1) * tf])
        act_ref[:, f * tf:(f + 1) * tf] = ((gate * _sigmoid(gate)) * up).astype(BF16)
        yield
    y = x + _dot(act_ref[...], wo_ref[...])
    if final:
        y = y * _rms_scale(y) * gf_ref[...]
    o_ref[...] = y


def _ffn_kernel(*refs, final, tf):
    _alternate(_ffn_stages(*refs, final=final, tf=tf))


def _ffn_in_specs(tm, d, dff):
    return [pl.BlockSpec((tm, d), lambda i, *_: (i, 0)), _resident((1, d)), _resident((d, 2 * dff)),
            _resident((dff, d)), _resident((1, d))]


def ffn(x, g, w_in, w_out, gf, *, tm, tf, final):
    t, d = x.shape
    dff = w_out.shape[0]
    return pl.pallas_call(
        functools.partial(_ffn_kernel, final=final, tf=tf),
        grid=(t // tm,),
        in_specs=_ffn_in_specs(tm, d, dff),
        out_specs=pl.BlockSpec((tm, d), lambda i: (i, 0)),
        out_shape=jax.ShapeDtypeStruct((t, d), F32),
        scratch_shapes=[pltpu.VMEM((tm, dff), BF16)],
        compiler_params=_cparams(("parallel",)),
        name="ffn",
    )(x, g.reshape(1, d), w_in, w_out, gf.reshape(1, d))


def _merge_kernel(x_ref, g_ref, op_ref, osb_ref, og_ref, wg_ref, wb_ref, wo_ref, o_ref):
    x = x_ref[...]
    d = x.shape[1]
    h = (x * _rms_scale(x) * g_ref[...]).astype(BF16)
    m = None
    row = 0
    for i, br_ref in enumerate((op_ref, osb_ref, og_ref)):
        width = br_ref.shape[1]
        branch = _dot(br_ref[...].astype(BF16), wb_ref[row:row + width, :])
        term = _sigmoid(_dot(h, wg_ref[:, i * d:(i + 1) * d])) * branch
        m = term if m is None else m + term
        row += width
    o_ref[...] = x + _dot(m.astype(BF16), wo_ref[...])


def merge(x, g, o_pool, o_sb, o_g, w_gate, w_branch, w_o, *, tm):
    t, d = x.shape
    return pl.pallas_call(
        _merge_kernel,
        grid=(t // tm,),
        in_specs=[pl.BlockSpec((tm, d), lambda i: (i, 0)), _resident((1, d)),
                  pl.BlockSpec((tm, POOL_W), lambda i: (i, 0)),
                  pl.BlockSpec((tm, SB_W), lambda i: (i, 0)),
                  pl.BlockSpec((tm, GDN_V_W), lambda i: (i, 0)),
                  _resident((d, 3 * d)), _resident((d, d)), _resident((d, d))],
        out_specs=pl.BlockSpec((tm, d), lambda i: (i, 0)),
        out_shape=jax.ShapeDtypeStruct((t, d), F32),
        compiler_params=_cparams(("parallel",)),
        name="merge",
    )(x, g.reshape(1, d), o_pool, o_sb, o_g, w_gate, w_branch, w_o)


def _pool_select(lane_grp, vals):
    out = vals[-1]
    for gi in range(len(vals) - 2, -1, -1):
        out = jnp.where(lane_grp == gi, vals[gi], out)
    return out


def _pool_prompt_kernel(u_ref, w_ref, sc_ref, o_ref):
    u = u_ref[0]
    t = lax.broadcasted_iota(jnp.int32, u.shape, 0)
    grp = lax.broadcasted_iota(jnp.int32, u.shape, 1) // POOL_GC

    def shifted(x, k):
        return jnp.where(t >= k, pltpu.roll(x, k, axis=0), 0.0)

    sums = []
    s, w = u, 1
    for win in POOL_WINDOWS:
        while w < win:
            s = s + shifted(s, w)
            w *= 2
        sums.append(s)
    sel = _pool_select(grp, sums)
    win = _pool_select(grp, [jnp.full(u.shape, wn, jnp.int32) for wn in POOL_WINDOWS])
    cnt = jnp.minimum(win, t + 1).astype(F32)
    pooled = sel / cnt - u
    o_ref[0] = _dot(pooled.astype(BF16), w_ref[...]) * sc_ref[...]


def pool_prompt(p3, w_bd, scale):
    b, l, _ = p3.shape
    return pl.pallas_call(
        _pool_prompt_kernel,
        grid=(b,),
        in_specs=[pl.BlockSpec((1, l, POOL_W), lambda i: (i, 0, P_POOL // POOL_W)),
                  pl.BlockSpec((POOL_W, POOL_W), lambda i: (0, 0)),
                  pl.BlockSpec((1, POOL_W), lambda i: (0, 0))],
        out_specs=pl.BlockSpec((1, l, POOL_W), lambda i: (i, 0, 0)),
        out_shape=jax.ShapeDtypeStruct((b, l, POOL_W), F32),
        compiler_params=_cparams(("parallel",)),
        name="pool_prompt",
    )(p3, w_bd, scale.reshape(1, POOL_W))


def _sb_log_terms(z, vis, suffix_mat):
    sp = _softplus(z)
    hi, lo = _split(sp if vis is None else jnp.where(vis, sp, 0.0))
    return sp, _dot(jnp.concatenate([hi, lo], axis=1), suffix_mat)


def _head_pair_rows(x, first_head_lanes):
    return jnp.concatenate([jnp.where(first_head_lanes, x, 0.0), jnp.where(first_head_lanes, 0.0, x)],
                           axis=0).astype(BF16)


def _sb_prompt_kernel(bias_ref, q_ref, k_ref, v_ref, u2_ref, o_ref, qb_ref, acc_ref, c_ref, *, tq):
    qi = pl.program_id(1)
    tk = SB_TK
    hw = 2 * SB_HEAD_DIM
    npair = SB_HEADS // 2
    n_diag = tq // tk
    head0 = lax.broadcasted_iota(jnp.int32, (tk, hw), 1) < SB_HEAD_DIM
    row = lax.broadcasted_iota(jnp.int32, (tq, tk), 0)
    col = lax.broadcasted_iota(jnp.int32, (tq, tk), 1)
    qb_ref[...] = (q_ref[0] * (SB_HEAD_DIM ** -0.5)).astype(BF16)
    acc_ref[...] = jnp.zeros_like(acc_ref)
    c_ref[...] = jnp.zeros_like(c_ref)
    suffix_mat = u2_ref[...]

    pairs = [slice(p * hw, (p + 1) * hw) for p in range(npair)]

    def key_blocks(blocks):
        z2 = [[_dot_nt(qb_ref[:, lanes], _head_pair_rows(k_ref[0, pl.ds(ks, tk), lanes], head0)) for lanes in pairs]
              for ks, _ in blocks]
        z = [[zb[h // 2][:, (h % 2) * tk:(h % 2 + 1) * tk] + bias_ref[h] for h in range(SB_HEADS)] for zb in z2]
        terms = [[_sb_log_terms(zh, vis, suffix_mat) for zh in zb] for zb, (_, vis) in zip(z, blocks)]
        a = [[None] * SB_HEADS for _ in blocks]
        for h in range(SB_HEADS):
            c = c_ref[h]
            for i, (_, vis) in enumerate(blocks):
                sp, s2 = terms[i][h]
                ah = jnp.exp(z[i][h] - sp - s2[:, :tk] - c)
                a[i][h] = (ah if vis is None else jnp.where(vis, ah, 0.0)).astype(BF16)
                c = c + s2[:, tk:]
            c_ref[h] = c
        for p, lanes in enumerate(pairs):
            upd = None
            for i, (ks, _) in enumerate(blocks):
                vv = _head_pair_rows(v_ref[0, pl.ds(ks, tk), lanes], head0)
                d = _dot(jnp.concatenate(a[i][2 * p:2 * p + 2], axis=1), vv)
                upd = d if upd is None else upd + d
            acc_ref[:, lanes] += upd

    key_blocks([(pl.multiple_of(qi * tq + d * tk, tk), (col + d * tk) < row) for d in range(n_diag - 1, -1, -1)])

    def full_body(jj, carry):
        last = qi * n_diag - 1 - jj * n_diag
        key_blocks([(pl.multiple_of((last - d) * tk, tk), None) for d in range(n_diag)])
        return carry

    lax.fori_loop(0, qi, full_body, 0)
    o_ref[0] = acc_ref[...]


def _suffix_matrix():
    j = np.arange(2 * SB_TK)[:, None] % SB_TK
    c = np.arange(2 * SB_TK)[None, :]
    return jnp.asarray(np.where(c < SB_TK, j > c, True), BF16)


def sb_prompt(p3, bias, *, tq):
    b, l, _ = p3.shape
    return pl.pallas_call(
        functools.partial(_sb_prompt_kernel, tq=tq),
        grid=(b, l // tq),
        in_specs=[pl.BlockSpec(memory_space=pltpu.SMEM),
                  pl.BlockSpec((1, tq, SB_W), lambda i, q: (i, q, P_SBQ // SB_W)),
                  pl.BlockSpec((1, l, SB_W), lambda i, q: (i, 0, P_SBK // SB_W)),
                  pl.BlockSpec((1, l, SB_W), lambda i, q: (i, 0, P_SBV // SB_W)),
                  pl.BlockSpec((2 * SB_TK, 2 * SB_TK), lambda i, q: (0, 0))],
        out_specs=pl.BlockSpec((1, tq, SB_W), lambda i, q: (i, q, 0)),
        out_shape=jax.ShapeDtypeStruct((b, l, SB_W), F32),
        scratch_shapes=[pltpu.VMEM((tq, SB_W), BF16), pltpu.VMEM((tq, SB_W), F32),
                        pltpu.VMEM((SB_HEADS, tq, SB_TK), F32)],
        compiler_params=_cparams(("parallel", "arbitrary")),
        name="sb_prompt",
    )(bias, p3, p3, p3, _suffix_matrix())


def _sb_own_head():
    return (lax.broadcasted_iota(jnp.int32, (SB_HEADS, SB_W), 0)
            == lax.broadcasted_iota(jnp.int32, (SB_HEADS, SB_W), 1) // SB_HEAD_DIM)


def _sb_decode_stages(q_ref, bias_ref, u2_ref, k_refs, v_refs, carry, acc, *, per_stage=4):
    nh = SB_HEADS
    group = len(k_refs)
    q = jnp.broadcast_to(q_ref[0] * (SB_HEAD_DIM ** -0.5), (nh, SB_W))
    qbd = jnp.where(_sb_own_head(), q, 0.0).astype(BF16)
    zs = []
    for g in range(group):
        zs.append(_dot(qbd, k_refs[g][...].astype(BF16)) + bias_ref[...])
        if g % per_stage == per_stage - 1:
            yield
    z = jnp.concatenate(zs, axis=0)
    sp, s2 = _sb_log_terms(z, None, u2_ref[...])
    yield
    later = [None] * group
    for g in range(group - 1, -1, -1):
        later[g] = carry
        carry = carry + s2[g * nh:(g + 1) * nh, SB_TK:]
    a = jnp.exp(z - sp - s2[:, :SB_TK] - jnp.concatenate(later, axis=0))
    yield
    for g in range(group):
        acc = acc + _dot_nt(a[g * nh:(g + 1) * nh].astype(BF16), v_refs[g][...].astype(BF16))
        if g % per_stage == per_stage - 1:
            yield
    return carry, acc


def _sb_decode_kernel(pt_ref, q_ref, bias_ref, u2_ref, *refs, group):
    k_refs, v_refs = refs[:group], refs[group:2 * group]
    o_ref, c_ref, acc_ref = refs[2 * group:]
    step = pl.program_id(1)

    @pl.when(step == 0)
    def _():
        c_ref[...] = jnp.zeros_like(c_ref)
        acc_ref[...] = jnp.zeros_like(acc_ref)

    (carry, acc), = _alternate(_sb_decode_stages(q_ref, bias_ref, u2_ref, k_refs, v_refs, c_ref[...], acc_ref[...]))
    c_ref[...] = carry
    acc_ref[...] = acc

    @pl.when(step == pl.num_programs(1) - 1)
    def _():
        o_ref[0] = jnp.sum(jnp.where(_sb_own_head(), acc, 0.0), axis=0, keepdims=True)


def sb_decode(q, cache_kt, cache_vt, page_table, bias, layer, *, group):
    bs, n_pages = page_table.shape
    page = cache_kt.shape[3]
    assert page == SB_TK and n_pages % group == 0
    n_steps = n_pages // group

    def page_spec(g):
        def index(b, s, pt):
            return (layer, pt[b * n_pages + (n_steps - 1 - s) * group + g], 0, 0)
        return pl.BlockSpec((None, None, SB_W, page), index)

    grid_spec = pltpu.PrefetchScalarGridSpec(
        num_scalar_prefetch=1,
        grid=(bs, n_steps),
        in_specs=[pl.BlockSpec((1, 1, SB_W), lambda b, s, pt: (b, 0, 0)),
                  pl.BlockSpec((SB_HEADS, 1), lambda b, s, pt: (0, 0)),
                  pl.BlockSpec((2 * SB_TK, 2 * SB_TK), lambda b, s, pt: (0, 0))]
        + [page_spec(g) for g in range(group)] * 2,
        out_specs=pl.BlockSpec((1, 1, SB_W), lambda b, s, pt: (b, 0, 0)),
        scratch_shapes=[pltpu.VMEM((SB_HEADS, SB_TK), F32), pltpu.VMEM((SB_HEADS, SB_W), F32)],
    )
    out = pl.pallas_call(
        functools.partial(_sb_decode_kernel, group=group),
        grid_spec=grid_spec,
        out_shape=jax.ShapeDtypeStruct((bs, 1, SB_W), F32),
        compiler_params=_cparams(("parallel", "arbitrary")),
        name="sb_decode",
    )(page_table.reshape(-1), q.reshape(bs, 1, SB_W), bias.reshape(SB_HEADS, 1), _suffix_matrix(),
      *([cache_kt] * group), *([cache_vt] * group))
    return out.reshape(bs, SB_W)


def _ffn_decode_kernel(pt_ref, x_ref, g_ref, wi_ref, wo_ref, gf_ref, q_ref, bias_ref, u2_ref, *refs, n_pages, final, tf):
    k_refs, v_refs = refs[:n_pages], refs[n_pages:2 * n_pages]
    o_ref, od_ref, act_ref = refs[2 * n_pages:]
    zero = lambda w: jnp.zeros((SB_HEADS, w), F32)
    _, (_, acc) = _alternate(
        _ffn_stages(x_ref, g_ref, wi_ref, wo_ref, gf_ref, o_ref, act_ref, final=final, tf=tf),
        _sb_decode_stages(q_ref, bias_ref, u2_ref, k_refs, v_refs, zero(SB_TK), zero(SB_W), per_stage=2))
    od_ref[0] = jnp.sum(jnp.where(_sb_own_head(), acc, 0.0), axis=0, keepdims=True)


def ffn_decode(x, g, w_in, w_out, gf, q, cache_kt, cache_vt, page_table, bias, layer, *, tf, final):
    t, d = x.shape
    dff = w_out.shape[0]
    bs, n_pages = page_table.shape
    page = cache_kt.shape[3]
    tm = t // bs
    assert page == SB_TK and tm * bs == t and tm % 8 == 0

    def page_spec(g):
        return pl.BlockSpec((None, None, SB_W, page), lambda i, pt: (layer, pt[i * n_pages + g], 0, 0))

    grid_spec = pltpu.PrefetchScalarGridSpec(
        num_scalar_prefetch=1,
        grid=(bs,),
        in_specs=_ffn_in_specs(tm, d, dff)
        + [pl.BlockSpec((1, 1, SB_W), lambda i, pt: (i, 0, 0)), _resident((SB_HEADS, 1)),
           _resident((2 * SB_TK, 2 * SB_TK))]
        + [page_spec(g) for g in range(n_pages)] * 2,
        out_specs=[pl.BlockSpec((tm, d), lambda i, pt: (i, 0)), pl.BlockSpec((1, 1, SB_W), lambda i, pt: (i, 0, 0))],
        scratch_shapes=[pltpu.VMEM((tm, dff), BF16)],
    )
    y, o = pl.pallas_call(
        functools.partial(_ffn_decode_kernel, n_pages=n_pages, final=final, tf=tf),
        grid_spec=grid_spec,
        out_shape=[jax.ShapeDtypeStruct((t, d), F32), jax.ShapeDtypeStruct((bs, 1, SB_W), F32)],
        compiler_params=_cparams(("parallel",)),
        name="ffn_decode",
    )(page_table.reshape(-1), x, g.reshape(1, d), w_in, w_out, gf.reshape(1, d), q.reshape(bs, 1, SB_W),
      bias.reshape(SB_HEADS, 1), _suffix_matrix(), *([cache_kt] * n_pages), *([cache_vt] * n_pages))
    return y, o.reshape(bs, SB_W)


def _head_indicator(width, head):
    i = np.arange(width) // head
    return jnp.asarray(i[:, None] == i[None, :], BF16)


def _gdn_qkv_post(conv_out, seg):
    act = conv_out * _sigmoid(conv_out)
    q = act[:, :GDN_QK_W]
    k = act[:, GDN_QK_W:2 * GDN_QK_W]
    v = act[:, 2 * GDN_QK_W:]
    qn = q * lax.rsqrt(_dot_exact_rhs(q * q, seg) + L2_EPS) * (GDN_DK ** -0.5)
    kn = k * lax.rsqrt(_dot_exact_rhs(k * k, seg) + L2_EPS)
    return qn, kn, v


def _gdn_gates(ab, alog, dtb):
    g = -jnp.exp(alog) * _softplus(ab + dtb)
    return g, _sigmoid(ab)


def _gdn_pre_kernel(x_ref, halo_ref, ab_ref, cw_ref, alog_ref, dtb_ref, seg_ref, q_ref, k_ref, v_ref, gb_ref):
    i = pl.program_id(1)
    x = x_ref[0]
    tl = x.shape[0]
    halo = jnp.where(i > 0, halo_ref[0], 0.0)
    nh = halo.shape[0]
    ext = jnp.concatenate([halo, x], axis=0)
    out = None
    for tap in range(GDN_CONV):
        d = GDN_CONV - 1 - tap
        src = x if d == 0 else pltpu.roll(ext, d, axis=0)[nh:]
        term = src * cw_ref[tap:tap + 1, :]
        out = term if out is None else out + term
    qn, kn, v = _gdn_qkv_post(out, seg_ref[...])
    q_ref[0] = qn
    k_ref[0] = kn
    v_ref[0] = v

    g, beta = _gdn_gates(ab_ref[0], alog_ref[...], dtb_ref[...])
    tt = lax.broadcasted_iota(jnp.int32, g.shape, 0) % GDN_CHUNK
    k = 1
    while k < GDN_CHUNK:
        g = g + jnp.where(tt >= k, pltpu.roll(g, k, axis=0), 0.0)
        k *= 2
    lane = lax.broadcasted_iota(jnp.int32, g.shape, 1)
    gb_ref[0] = jnp.where(lane < GDN_HEADS, g, beta)


def gdn_pre(p3, conv_w, alog, dtb, *, tl):
    b, l, _ = p3.shape
    nh = 8
    out_sd = jax.ShapeDtypeStruct((b, l, GDN_QK_W), F32)
    return pl.pallas_call(
        _gdn_pre_kernel,
        grid=(b, l // tl),
        in_specs=[pl.BlockSpec((1, tl, GDN_QKV_W), lambda n, i: (n, i, P_GQKV // GDN_QKV_W)),
                  pl.BlockSpec((1, nh, GDN_QKV_W), lambda n, i: (n, jnp.maximum(i * (tl // nh) - 1, 0), 0)),
                  pl.BlockSpec((1, tl, LANES), lambda n, i: (n, i, P_AB // LANES)),
                  pl.BlockSpec((GDN_CONV, GDN_QKV_W), lambda n, i: (0, 0)),
                  pl.BlockSpec((1, LANES), lambda n, i: (0, 0)),
                  pl.BlockSpec((1, LANES), lambda n, i: (0, 0)),
                  pl.BlockSpec((GDN_QK_W, GDN_QK_W), lambda n, i: (0, 0))],
        out_specs=[pl.BlockSpec((1, tl, GDN_QK_W), lambda n, i: (n, i, 0))] * 3
        + [pl.BlockSpec((1, tl, LANES), lambda n, i: (n, i, 0))],
        out_shape=[out_sd, out_sd, out_sd, jax.ShapeDtypeStruct((b, l, LANES), F32)],
        compiler_params=_cparams(("parallel", "parallel")),
        name="gdn_pre",
    )(p3, p3, p3, conv_w, alog, dtb, _head_indicator(GDN_QK_W, GDN_DK))


def _gate_broadcast_matrix():
    r = np.arange(2 * LANES)[:, None] % LANES
    c = np.arange(2 * LANES)[None, :]
    return jnp.asarray(np.where(c < LANES, r < GDN_HEADS, (r >= GDN_HEADS) & (r < 2 * GDN_HEADS)), BF16)


def _gdn_chunk_terms(q, k, v, gb, e_mat):
    ch = GDN_CHUNK
    r = GDN_HEADS * ch
    sh = ch.bit_length() - 1
    chunks = range(len(q))
    ri = lax.broadcasted_iota(jnp.int32, (r, r), 0)
    ci = lax.broadcasted_iota(jnp.int32, (r, r), 1)
    same_head = (ri >> sh) == (ci >> sh)
    incl = same_head & (ri >= ci)
    strict = same_head & (ri > ci)
    eye = (ri == ci).astype(F32)

    def stack(x):
        return jnp.where(same_head, jnp.concatenate([x] * GDN_HEADS, axis=0), 0.0)

    def widen(x):
        return jnp.concatenate([x] * (r // LANES), axis=1)

    kbd, qbd, vbd = [stack(x) for x in k], [stack(x) for x in q], [stack(x) for x in v]
    yield

    l1 = lax.broadcasted_iota(jnp.int32, (r, LANES), 1)
    r1 = lax.broadcasted_iota(jnp.int32, (r, LANES), 0) >> sh
    gate_lane = (l1 == r1) | (l1 == r1 + GDN_HEADS)
    gsel = [_split(jnp.where(gate_lane, jnp.concatenate([x] * GDN_HEADS, axis=0), 0.0)) for x in gb]
    gbc = [_dot(jnp.concatenate([hi, lo], axis=1), e_mat) for hi, lo in gsel]
    g1 = [x[:, :LANES] for x in gbc]
    gl1 = [jnp.concatenate([jnp.broadcast_to(x[(h + 1) * ch - 1:(h + 1) * ch, :], (ch, LANES))
                            for h in range(GDN_HEADS)], axis=0) for x in g1]
    gcol = [widen(x) for x in g1]
    bcol = [widen(x[:, LANES:]) for x in gbc]
    eg = [widen(jnp.exp(x)) for x in g1]
    egl = [widen(jnp.exp(x)) for x in gl1]
    ekd = [widen(jnp.exp(gl1[i] - g1[i])) for i in chunks]

    yield
    decay = [jnp.where(incl, jnp.exp(jnp.where(incl, x - x.T, 0.0)), 0.0) for x in gcol]
    kb = [kbd[i] * bcol[i] for i in chunks]
    kbd_b = [x.astype(BF16) for x in kbd]
    a_mat = [jnp.where(strict, _dot_nt(kb[i].astype(BF16), kbd_b[i]) * decay[i], 0.0) for i in chunks]
    yield

    t_inv = [eye - jnp.where((ri >> 1) == (ci >> 1), x, 0.0) for x in a_mat]
    for lv in range(1, sh):
        off = ((ri >> (lv + 1)) == (ci >> (lv + 1))) & ((ri >> lv) != (ci >> lv))
        tb = [x.astype(BF16) for x in t_inv]
        m = [_dot(jnp.where(off, a_mat[i], 0.0).astype(BF16), tb[i]).astype(BF16) for i in chunks]
        yield
        t_inv = [t_inv[i] - _dot(tb[i], m[i]) for i in chunks]
        yield
    res = [(eye - t_inv[i]) - _mm3(a_mat[i], t_inv[i]) for i in chunks]
    yield
    t_inv = [t_inv[i] + _dot(t_inv[i].astype(BF16), res[i].astype(BF16)) for i in chunks]
    t_split = [_split(x) for x in t_inv]
    yield

    def apply_t(i, x):
        th, tl = t_split[i]
        xh, xl = _split(x)
        return _dot(th, xh) + (_dot(th, xl) + _dot(tl, xh))

    u = [apply_t(i, vbd[i] * bcol[i]) for i in chunks]
    yield
    w = [apply_t(i, kb[i] * eg[i]).astype(BF16) for i in chunks]
    yield
    qk = [jnp.where(incl, _dot_nt(qbd[i].astype(BF16), kbd_b[i]) * decay[i], 0.0).astype(BF16) for i in chunks]
    qd = [(qbd[i] * eg[i]).astype(BF16) for i in chunks]
    kd = [(kbd[i] * ekd[i]).astype(BF16) for i in chunks]
    return list(zip(u, w, qk, qd, kd, egl))


def _gdn_chunk_kernel(q_ref, k_ref, v_ref, gb_ref, z_ref, gn_ref, e_ref, seg_ref, og_ref, s_ref, st_ref, *, nc):
    c = pl.program_id(1)
    ch = GDN_CHUNK

    @pl.when(c == 0)
    def _():
        st_ref[...] = jnp.zeros_like(st_ref)

    def terms_of(chunk_ids):
        rows = [slice(i * ch, (i + 1) * ch) for i in chunk_ids]
        return _gdn_chunk_terms([q_ref[0, x, :] for x in rows], [k_ref[0, x, :] for x in rows],
                                [v_ref[0, x, :] for x in rows], [gb_ref[0, x, :] for x in rows], e_ref[...])

    state = {"s": st_ref[...]}
    outs = []

    def recurrence(terms):
        for u, w, qk, qd, kd, egl in terms:
            sb = state["s"].astype(BF16)
            vnb = (u - _dot(w, sb)).astype(BF16)
            o = _dot(qd, sb)
            yield
            o = o + _dot(qk, vnb)
            state["s"] = state["s"] * egl + _dot_tn(kd, vnb)
            oc = o[0:ch]
            for h in range(1, GDN_HEADS):
                oc = oc + o[h * ch:(h + 1) * ch]
            outs.append(oc)
            yield

    half = max(nc // 2, 1)
    terms_a, = _alternate(terms_of(range(half)))
    if nc > half:
        terms_b, _ = _alternate(terms_of(range(half, nc)), recurrence(terms_a))
        _alternate(recurrence(terms_b))
    else:
        _alternate(recurrence(terms_a))
    s = state["s"]
    st_ref[...] = s

    oc = jnp.concatenate(outs, axis=0)
    ms = _dot_exact_rhs(oc * oc, seg_ref[...]) * (1.0 / GDN_DV)
    zz = z_ref[0]
    og_ref[0] = oc * lax.rsqrt(ms + RMS_EPS) * gn_ref[...] * (zz * _sigmoid(zz))

    @pl.when(c == pl.num_programs(1) - 1)
    def _():
        for h in range(GDN_HEADS):
            s_ref[0, h] = s[h * GDN_DK:(h + 1) * GDN_DK, h * GDN_DV:(h + 1) * GDN_DV]


def gdn_chunk(qn, kn, v, gb, p3, gnorm, *, nc):
    b, l, _ = qn.shape
    ch = GDN_CHUNK * nc
    r = GDN_HEADS * GDN_CHUNK
    blk = lambda w: pl.BlockSpec((1, ch, w), lambda n, c: (n, c, 0))
    return pl.pallas_call(
        functools.partial(_gdn_chunk_kernel, nc=nc),
        grid=(b, l // ch),
        in_specs=[blk(GDN_QK_W), blk(GDN_QK_W), blk(GDN_V_W), blk(LANES),
                  pl.BlockSpec((1, ch, GDN_V_W), lambda n, c: (n, c, P_GZ // GDN_V_W)),
                  pl.BlockSpec((1, GDN_V_W), lambda n, c: (0, 0)),
                  pl.BlockSpec((2 * LANES, 2 * LANES), lambda n, c: (0, 0)),
                  pl.BlockSpec((GDN_V_W, GDN_V_W), lambda n, c: (0, 0))],
        out_specs=[pl.BlockSpec((1, ch, GDN_V_W), lambda n, c: (n, c, 0)),
                   pl.BlockSpec((1, GDN_HEADS, GDN_DK, GDN_DV), lambda n, c: (n, 0, 0, 0))],
        out_shape=[jax.ShapeDtypeStruct((b, l, GDN_V_W), F32),
                   jax.ShapeDtypeStruct((b, GDN_HEADS, GDN_DK, GDN_DV), F32)],
        scratch_shapes=[pltpu.VMEM((r, r), F32)],
        compiler_params=_cparams(("parallel", "arbitrary")),
        name="gdn_chunk",
    )(qn, kn, v, gb, p3, jnp.tile(gnorm, GDN_HEADS).reshape(1, GDN_V_W), _gate_broadcast_matrix(),
      _head_indicator(GDN_V_W, GDN_DV))


def _sample_pre_kernel(up_ref, x_ref, ab_ref, ph_ref, ch_ref, pw_ref, psc_ref, cw_ref, alog_ref, dtb_ref, seg_ref,
                       op_ref, q_ref, k_ref, v_ref, gb_ref, *, pos0):
    u = up_ref[...]
    ph = ph_ref[...]
    hist = ph.shape[1]
    ti = lax.broadcasted_iota(jnp.int32, ph.shape, 1)
    grp = lax.broadcasted_iota(jnp.int32, u.shape, 1) // POOL_GC
    means = []
    for win in POOL_WINDOWS:
        n_hist = min(win - 1, hist)
        tail = jnp.sum(jnp.where(ti >= hist - n_hist, ph, 0.0), axis=1)
        means.append((tail + u) / float(min(win, pos0 + 1)))
    pooled = _pool_select(grp, means) - u
    op_ref[...] = _dot(pooled.astype(BF16), pw_ref[...]) * psc_ref[...]

    cw = cw_ref[...]
    out = jnp.sum(ch_ref[...] * cw[None, :GDN_CONV - 1, :], axis=1) + x_ref[...] * cw[GDN_CONV - 1:, :]
    qn, kn, v = _gdn_qkv_post(out, seg_ref[...])
    q_ref[...] = qn
    k_ref[...] = kn
    v_ref[...] = v

    g, beta = _gdn_gates(ab_ref[...], alog_ref[...], dtb_ref[...])
    lane = lax.broadcasted_iota(jnp.int32, g.shape, 1)
    gb_ref[...] = jnp.where(lane < GDN_HEADS, jnp.exp(g), beta)


def sample_pre(p, pool_hist, conv_hist, pool_w_bd, pool_scale, conv_w, alog, dtb, *, pos0):
    bs = p.shape[0]
    full = lambda shape: pl.BlockSpec(shape, lambda i: (0,) * len(shape))
    sd = lambda w: jax.ShapeDtypeStruct((bs, w), F32)
    return pl.pallas_call(
        functools.partial(_sample_pre_kernel, pos0=pos0),
        grid=(1,),
        in_specs=[pl.BlockSpec((bs, POOL_W), lambda i: (0, P_POOL // POOL_W)),
                  pl.BlockSpec((bs, GDN_QKV_W), lambda i: (0, P_GQKV // GDN_QKV_W)),
                  pl.BlockSpec((bs, LANES), lambda i: (0, P_AB // LANES)),
                  full(pool_hist.shape), full(conv_hist.shape), full((POOL_W, POOL_W)), full((1, POOL_W)),
                  full((GDN_CONV, GDN_QKV_W)), full((1, LANES)), full((1, LANES)), full((GDN_QK_W, GDN_QK_W))],
        out_specs=[full((bs, POOL_W)), full((bs, GDN_QK_W)), full((bs, GDN_QK_W)), full((bs, GDN_V_W)),
                   full((bs, LANES))],
        out_shape=[sd(POOL_W), sd(GDN_QK_W), sd(GDN_QK_W), sd(GDN_V_W), sd(LANES)],
        compiler_params=_cparams(("arbitrary",)),
        name="sample_pre",
    )(p, p, p, pool_hist, conv_hist, pool_w_bd, pool_scale.reshape(1, POOL_W), conv_w, alog, dtb,
      _head_indicator(GDN_QK_W, GDN_DK))


def _gdn_step_kernel(s_ref, k_ref, q_ref, v_ref, z_ref, eg_ref, beta_ref, gn_ref, so_ref, og_ref):
    k = k_ref[...]
    s = s_ref[...] * eg_ref[...]
    delta = (v_ref[...] - jnp.sum(k * s, axis=2, keepdims=True)) * beta_ref[...]
    s = s + k * delta
    so_ref[...] = s
    o = jnp.sum(q_ref[...] * s, axis=2, keepdims=True)
    zz = z_ref[...]
    og_ref[...] = o * _rms_scale(o) * gn_ref[...] * (zz * _sigmoid(zz))


def gdn_step(state, kn, qn, v, z, eg, beta, gnorm, *, bt):
    bs = state.shape[0]
    h, dk, dv = GDN_HEADS, GDN_DK, GDN_DV
    blk = lambda a, c: pl.BlockSpec((bt, h, a, c), lambda i: (i, 0, 0, 0))
    return pl.pallas_call(
        _gdn_step_kernel,
        grid=(bs // bt,),
        in_specs=[blk(dk, dv), blk(dk, 1), blk(dk, 1), blk(1, dv), blk(1, dv), blk(1, 1), blk(1, 1),
                  pl.BlockSpec((1, 1, 1, dv), lambda i: (0, 0, 0, 0))],
        out_specs=[blk(dk, dv), blk(1, dv)],
        out_shape=[jax.ShapeDtypeStruct((bs, h, dk, dv), F32), jax.ShapeDtypeStruct((bs, h, 1, dv), F32)],
        compiler_params=_cparams(("parallel",)),
        name="gdn_step",
    )(state, kn.reshape(bs, h, dk, 1), qn.reshape(bs, h, dk, 1), v.reshape(bs, h, 1, dv), z.reshape(bs, h, 1, dv),
      eg.reshape(bs, h, 1, 1), beta.reshape(bs, h, 1, 1), gnorm.reshape(1, 1, 1, dv))


def _largest_tile(n, cap):
    t = min(n, cap)
    while n % t:
        t //= 2
    return t


def _pack_w_in(w_in):
    depth, d, _ = w_in.shape
    pad = jnp.zeros((depth, d, P_W - P_AB - (R_GZ - R_GA)), w_in.dtype)
    return jnp.concatenate([w_in[..., R_GQKV:R_GA], w_in[..., R_POOL:R_GQKV], w_in[..., R_GZ:R_GATE],
                            w_in[..., R_GA:R_GZ], pad], axis=-1).astype(BF16)


def _lane_pad(x):
    return jnp.pad(x, ((0, 0), (0, LANES - x.shape[1])))


def kernel(x_prompt, x_sample, cache_sb_k, cache_sb_v, page_table, state_pool, state_gdn_conv, state_gdn, norm1_g, w_in, pool_w, pool_scale, sb_bias, gdn_conv_w, gdn_a_log, gdn_dt_bias, gdn_norm_g, w_branch, w_o, norm2_g, w_ffn_in, w_ffn_out, normf_g):
    bp, seq, d = x_prompt.shape
    bs = x_sample.shape[0]
    depth = w_in.shape[0]
    n_phys, page = cache_sb_k.shape[1], cache_sb_k.shape[2]
    past_len = page_table.shape[1] * page
    tp = bp * seq

    w_in_p = _pack_w_in(w_in)
    w_gate_b = w_in[..., R_GATE:R_END].astype(BF16)
    w_branch_b = w_branch.astype(BF16)
    w_o_b = w_o.astype(BF16)
    w_ffn_in_b = w_ffn_in.astype(BF16)
    w_ffn_out_b = w_ffn_out.astype(BF16)
    eye_g = jnp.eye(len(POOL_WINDOWS), dtype=F32)
    pool_w_bd = jnp.einsum("lgcd,gh->lgchd", pool_w, eye_g).reshape(depth, POOL_W, POOL_W).astype(BF16)
    alog_p = _lane_pad(gdn_a_log)
    dtb_p = _lane_pad(gdn_dt_bias)
    cache_kt = jnp.transpose(cache_sb_k, (0, 1, 3, 4, 2)).reshape(depth, n_phys, SB_W, page)
    cache_vt = jnp.transpose(cache_sb_v, (0, 1, 3, 4, 2)).reshape(depth, n_phys, SB_W, page)
    n_pages = page_table.shape[1]
    page_group = _largest_tile(n_pages, 16)
    fuse_decode = page_group == n_pages and tp % bs == 0 and (tp // bs) % 8 == 0

    tm_p = _largest_tile(tp, 512)
    tm_merge = _largest_tile(tp, 512)
    tm_s = _largest_tile(bs, 128)
    tq = _largest_tile(seq, 256)
    tl = _largest_tile(seq, 256)
    bt = _largest_tile(bs, 8)

    xp = x_prompt.reshape(tp, d)
    xs = x_sample.reshape(bs, d)
    outs = [[] for _ in range(8)]
    kt_new = vt_new = None
    for l in range(depth):
        final = l == depth - 1
        p, kt_new, vt_new = rms_matmul(xp, norm1_g[l], w_in_p[l], tm=tm_p, tn=1024,
                                       kv_t=(l, depth, seq, kt_new, vt_new))
        p3 = p.reshape(bp, seq, P_W)
        o_pool = pool_prompt(p3, pool_w_bd[l], pool_scale[l])
        o_sb = sb_prompt(p3, sb_bias[l], tq=tq)
        qn, kn, gv, gb = gdn_pre(p3, gdn_conv_w[l], alog_p[l:l + 1], dtb_p[l:l + 1], tl=tl)
        o_g, s_p = gdn_chunk(qn, kn, gv, gb, p3, gdn_norm_g[l], nc=_largest_tile(seq // GDN_CHUNK, 8))
        xp = merge(xp, norm1_g[l], o_pool.reshape(tp, POOL_W), o_sb.reshape(tp, SB_W), o_g.reshape(tp, GDN_V_W),
                   w_gate_b[l], w_branch_b[l], w_o_b[l], tm=tm_merge)
        ps = rms_matmul(xs, norm1_g[l], w_in_p[l], tm=tm_s, tn=1024)
        if fuse_decode:
            xp, o_sb_s = ffn_decode(xp, norm2_g[l], w_ffn_in_b[l], w_ffn_out_b[l], normf_g, ps[:, P_SBQ:P_SBK],
                                    cache_kt, cache_vt, page_table, sb_bias[l], l, tf=256, final=final)
        else:
            xp = ffn(xp, norm2_g[l], w_ffn_in_b[l], w_ffn_out_b[l], normf_g, tm=tm_p, tf=256, final=final)
            o_sb_s = sb_decode(ps[:, P_SBQ:P_SBK], cache_kt, cache_vt, page_table, sb_bias[l], l, group=page_group)
        outs[0].append(p3[:, seq - POOL_HIST:, P_POOL:P_SBQ])
        outs[1].append(p3[:, seq - (GDN_CONV - 1):, P_GQKV:P_POOL])
        outs[2].append(s_p)

        o_pool_s, qn_s, kn_s, gv_s, gb_s = sample_pre(ps, state_pool[l], state_gdn_conv[l], pool_w_bd[l],
                                                      pool_scale[l], gdn_conv_w[l], alog_p[l:l + 1],
                                                      dtb_p[l:l + 1], pos0=past_len)
        s_s, o_g_s = gdn_step(state_gdn[l], kn_s, qn_s, gv_s, ps[:, P_GZ:P_AB], gb_s[:, :GDN_HEADS],
                              gb_s[:, GDN_HEADS:2 * GDN_HEADS], gdn_norm_g[l], bt=bt)
        xs = merge(xs, norm1_g[l], o_pool_s, o_sb_s, o_g_s.reshape(bs, GDN_V_W), w_gate_b[l], w_branch_b[l],
                   w_o_b[l], tm=tm_s)
        xs = ffn(xs, norm2_g[l], w_ffn_in_b[l], w_ffn_out_b[l], normf_g, tm=tm_s, tf=256, final=final)
        outs[3].append(ps[:, P_SBK:P_SBV].reshape(bs, 1, SB_HEADS, SB_HEAD_DIM))
        outs[4].append(ps[:, P_SBV:P_GZ].reshape(bs, 1, SB_HEADS, SB_HEAD_DIM))
        outs[5].append(jnp.concatenate([state_pool[l], ps[:, None, P_POOL:P_SBQ]], axis=1)[:, 1:])
        outs[6].append(jnp.concatenate([state_gdn_conv[l], ps[:, None, P_GQKV:P_POOL]], axis=1)[:, 1:])
        outs[7].append(s_s)

    y_prompt = xp.reshape(bp, seq, d)
    y_sample = xs.reshape(bs, 1, d)
    new_kv = [jnp.transpose(t.reshape(depth, bp, SB_HEADS, SB_HEAD_DIM, seq), (0, 1, 4, 2, 3))
              for t in (kt_new, vt_new)]
    return (y_prompt, y_sample, *new_kv) + tuple(jnp.stack(o) for o in outs)
```

```python
import functools

import numpy as np
import jax
import jax.numpy as jnp
from jax import lax
from jax.experimental import pallas as pl
from jax.experimental.pallas import tpu as pltpu

F32 = jnp.float32
BF16 = jnp.bfloat16

D_MODEL = 1024
POOL_WINDOWS = (2, 4, 8, 16)
POOL_W = 256
POOL_GC = 64
POOL_HIST = 15
SB_HEADS = 8
SB_HEAD_DIM = 64
SB_W = 512
GDN_HEADS = 4
GDN_DK = 64
GDN_DV = 64
GDN_QK_W = 256
GDN_V_W = 256
GDN_QKV_W = 768
GDN_CONV = 4
GDN_CHUNK = 64
D_FF = 2816
RMS_EPS = 1e-6
L2_EPS = 1e-6

R_POOL, R_SBQ, R_GQKV, R_GA, R_GZ, R_GATE, R_END = 0, 256, 1792, 2560, 2568, 2824, 5896
P_GQKV, P_POOL, P_SBQ, P_SBK, P_SBV, P_GZ, P_AB, P_W = 0, 768, 1024, 1536, 2048, 2560, 2816, 3072

LANES = 128
SB_TK = LANES
VMEM_LIMIT = 56 * 1024 * 1024


def _cparams(sem):
    return pltpu.CompilerParams(dimension_semantics=sem, vmem_limit_bytes=VMEM_LIMIT)


def _sigmoid(x):
    return 1.0 / (1.0 + jnp.exp(-x))


def _softplus(x):
    return jnp.maximum(x, 0.0) + jnp.log(1.0 + jnp.exp(-jnp.abs(x)))


def _split(x):
    hi = x.astype(BF16)
    lo = (x - hi.astype(F32)).astype(BF16)
    return hi, lo


def _dot(a, b):
    return jnp.dot(a, b, preferred_element_type=F32)


def _dot_nt(a, b):
    return lax.dot_general(a, b, (((1,), (1,)), ((), ())), preferred_element_type=F32)


def _dot_tn(a, b):
    return lax.dot_general(a, b, (((0,), (0,)), ((), ())), preferred_element_type=F32)


def _mm3(a, b):
    ah, al = _split(a)
    bh, bl = _split(b)
    return _dot(ah, bh) + (_dot(ah, bl) + _dot(al, bh))


def _dot_exact_rhs(x, m):
    hi, lo = _split(x)
    return _dot(hi, m) + _dot(lo, m)


def _rms_scale(x):
    return lax.rsqrt(jnp.mean(x * x, axis=-1, keepdims=True) + RMS_EPS)


def _alternate(*stage_gens):
    results = [None] * len(stage_gens)
    live = list(range(len(stage_gens)))
    while live:
        for i in list(live):
            try:
                next(stage_gens[i])
            except StopIteration as done:
                results[i] = done.value
                live.remove(i)
    return results


def _resident(shape, layer=None):
    if layer is None:
        return pl.BlockSpec(shape, lambda *_: (0,) * len(shape), pipeline_mode=pl.Buffered(1))
    return pl.BlockSpec((None, *shape), lambda *_: (layer,) + (0,) * len(shape), pipeline_mode=pl.Buffered(1))


def _rms_matmul_kernel(x_ref, g_ref, w_ref, *refs, tn, kv_t):
    o_ref = refs[-3] if kv_t else refs[-1]
    x = x_ref[...]
    h = (x * _rms_scale(x) * g_ref[...]).astype(BF16)
    for j in range(w_ref.shape[1] // tn):
        o_ref[:, j * tn:(j + 1) * tn] = _dot(h, w_ref[:, j * tn:(j + 1) * tn])
    if kv_t:
        kt_ref, vt_ref = refs[-2:]
        kt_ref[0] = o_ref[:, P_SBK:P_SBV].T
        vt_ref[0] = o_ref[:, P_SBV:P_GZ].T


def rms_matmul(x, g, w, layer, *, tm, tn, kv_t=None):
    t, d = x.shape
    n = w.shape[2]
    in_specs = [pl.BlockSpec((tm, d), lambda i: (i, 0)), _resident((1, d)), _resident((d, n), layer)]
    out_specs = [pl.BlockSpec((tm, n), lambda i: (i, 0))]
    out_shape = [jax.ShapeDtypeStruct((t, n), F32)]
    operands = [x, g.reshape(1, d), w]
    aliases = {}
    if kv_t:
        depth, seq, k_buf, v_buf = kv_t
        tiles = seq // tm
        t_spec = pl.BlockSpec((None, 1, SB_W, tm), lambda i: (layer, i // tiles, 0, i % tiles))
        out_specs += [t_spec, t_spec]
        out_shape += [jax.ShapeDtypeStruct((depth, t // seq, SB_W, seq), F32)] * 2
        if k_buf is not None:
            in_specs += [pl.BlockSpec(memory_space=pl.ANY)] * 2
            operands += [k_buf, v_buf]
            aliases = {3: 1, 4: 2}
    out = pl.pallas_call(
        functools.partial(_rms_matmul_kernel, tn=tn, kv_t=bool(kv_t)),
        grid=(t // tm,),
        in_specs=in_specs,
        out_specs=out_specs,
        out_shape=out_shape,
        input_output_aliases=aliases,
        compiler_params=_cparams(("parallel",)),
        name="in_proj",
    )(*operands)
    return out if kv_t else out[0]


def _ffn_stages(x_ref, g_ref, wi_ref, wo_ref, gf_ref, o_ref, act_ref, *, final, tf):
    x = x_ref[...]
    h = (x * _rms_scale(x) * g_ref[...]).astype(BF16)
    dff = wo_ref.shape[0]
    for f in range(dff // tf):
        gate = _dot(h, wi_ref[:, f * tf:(f + 1) * tf])
        up = _dot(h, wi_ref[:, dff + f * tf:dff + (f + 1) * tf])
        act_ref[:, f * tf:(f + 1) * tf] = ((gate * _sigmoid(gate)) * up).astype(BF16)
        yield
    y = x + _dot(act_ref[...], wo_ref[...])
    if final:
        y = y * _rms_scale(y) * gf_ref[...]
    o_ref[...] = y


def _ffn_kernel(*refs, final, tf):
    _alternate(_ffn_stages(*refs, final=final, tf=tf))


def _ffn_in_specs(tm, d, dff, layer):
    return [pl.BlockSpec((tm, d), lambda i, *_: (i, 0)), _resident((1, d)), _resident((d, 2 * dff), layer),
            _resident((dff, d), layer), _resident((1, d))]


def ffn(x, g, w_in, w_out, gf, layer, *, tm, tf, final):
    t, d = x.shape
    dff = w_out.shape[1]
    return pl.pallas_call(
        functools.partial(_ffn_kernel, final=final, tf=tf),
        grid=(t // tm,),
        in_specs=_ffn_in_specs(tm, d, dff, layer),
        out_specs=pl.BlockSpec((tm, d), lambda i: (i, 0)),
        out_shape=jax.ShapeDtypeStruct((t, d), F32),
        scratch_shapes=[pltpu.VMEM((tm, dff), BF16)],
        compiler_params=_cparams(("parallel",)),
        name="ffn",
    )(x, g.reshape(1, d), w_in, w_out, gf.reshape(1, d))


def _merge_kernel(x_ref, g_ref, op_ref, osb_ref, og_ref, wg_ref, wb_ref, wo_ref, o_ref):
    x = x_ref[...]
    d = x.shape[1]
    h = (x * _rms_scale(x) * g_ref[...]).astype(BF16)
    m = None
    row = 0
    for i, br_ref in enumerate((op_ref, osb_ref, og_ref)):
        width = br_ref.shape[1]
        branch = _dot(br_ref[...].astype(BF16), wb_ref[row:row + width, :])
        term = _sigmoid(_dot(h, wg_ref[:, i * d:(i + 1) * d])) * branch
        m = term if m is None else m + term
        row += width
    o_ref[...] = x + _dot(m.astype(BF16), wo_ref[...])


def merge(x, g, o_pool, o_sb, o_g, w_gate, w_branch, w_o, layer, *, tm):
    t, d = x.shape
    return pl.pallas_call(
        _merge_kernel,
        grid=(t // tm,),
        in_specs=[pl.BlockSpec((tm, d), lambda i: (i, 0)), _resident((1, d)),
                  pl.BlockSpec((tm, POOL_W), lambda i: (i, 0)),
                  pl.BlockSpec((tm, SB_W), lambda i: (i, 0)),
                  pl.BlockSpec((tm, GDN_V_W), lambda i: (i, 0)),
                  _resident((d, 3 * d), layer), _resident((d, d), layer), _resident((d, d), layer)],
        out_specs=pl.BlockSpec((tm, d), lambda i: (i, 0)),
        out_shape=jax.ShapeDtypeStruct((t, d), F32),
        compiler_params=_cparams(("parallel",)),
        name="merge",
    )(x, g.reshape(1, d), o_pool, o_sb, o_g, w_gate, w_branch, w_o)


def _pool_select(lane_grp, vals):
    out = vals[-1]
    for gi in range(len(vals) - 2, -1, -1):
        out = jnp.where(lane_grp == gi, vals[gi], out)
    return out


def _pool_prompt_kernel(u_ref, w_ref, sc_ref, o_ref):
    u = u_ref[0]
    t = lax.broadcasted_iota(jnp.int32, u.shape, 0)
    grp = lax.broadcasted_iota(jnp.int32, u.shape, 1) // POOL_GC

    def shifted(x, k):
        return jnp.where(t >= k, pltpu.roll(x, k, axis=0), 0.0)

    sums = []
    s, w = u, 1
    for win in POOL_WINDOWS:
        while w < win:
            s = s + shifted(s, w)
            w *= 2
        sums.append(s)
    sel = _pool_select(grp, sums)
    win = _pool_select(grp, [jnp.full(u.shape, wn, jnp.int32) for wn in POOL_WINDOWS])
    cnt = jnp.minimum(win, t + 1).astype(F32)
    pooled = sel / cnt - u
    o_ref[0] = _dot(pooled.astype(BF16), w_ref[...]) * sc_ref[...]


def pool_prompt(p3, w_bd, scale):
    b, l, _ = p3.shape
    return pl.pallas_call(
        _pool_prompt_kernel,
        grid=(b,),
        in_specs=[pl.BlockSpec((1, l, POOL_W), lambda i: (i, 0, P_POOL // POOL_W)),
                  pl.BlockSpec((POOL_W, POOL_W), lambda i: (0, 0)),
                  pl.BlockSpec((1, POOL_W), lambda i: (0, 0))],
        out_specs=pl.BlockSpec((1, l, POOL_W), lambda i: (i, 0, 0)),
        out_shape=jax.ShapeDtypeStruct((b, l, POOL_W), F32),
        compiler_params=_cparams(("parallel",)),
        name="pool_prompt",
    )(p3, w_bd, scale.reshape(1, POOL_W))


def _sb_log_terms(z, vis, suffix_mat):
    sp = _softplus(z)
    hi, lo = _split(sp if vis is None else jnp.where(vis, sp, 0.0))
    return sp, _dot(jnp.concatenate([hi, lo], axis=1), suffix_mat)


def _head_pair_rows(x, first_head_lanes):
    return jnp.concatenate([jnp.where(first_head_lanes, x, 0.0), jnp.where(first_head_lanes, 0.0, x)],
                           axis=0).astype(BF16)


def _sb_prompt_kernel(bias_ref, q_ref, k_ref, v_ref, u2_ref, o_ref, qb_ref, acc_ref, c_ref, *, tq):
    qi = pl.program_id(1)
    tk = SB_TK
    hw = 2 * SB_HEAD_DIM
    npair = SB_HEADS // 2
    n_diag = tq // tk
    head0 = lax.broadcasted_iota(jnp.int32, (tk, hw), 1) < SB_HEAD_DIM
    row = lax.broadcasted_iota(jnp.int32, (tq, tk), 0)
    col = lax.broadcasted_iota(jnp.int32, (tq, tk), 1)
    qb_ref[...] = (q_ref[0] * (SB_HEAD_DIM ** -0.5)).astype(BF16)
    acc_ref[...] = jnp.zeros_like(acc_ref)
    c_ref[...] = jnp.zeros_like(c_ref)
    suffix_mat = u2_ref[...]

    pairs = [slice(p * hw, (p + 1) * hw) for p in range(npair)]

    def key_blocks(blocks):
        z2 = [[_dot_nt(qb_ref[:, lanes], _head_pair_rows(k_ref[0, pl.ds(ks, tk), lanes], head0)) for lanes in pairs]
              for ks, _ in blocks]
        z = [[zb[h // 2][:, (h % 2) * tk:(h % 2 + 1) * tk] + bias_ref[h] for h in range(SB_HEADS)] for zb in z2]
        terms = [[_sb_log_terms(zh, vis, suffix_mat) for zh in zb] for zb, (_, vis) in zip(z, blocks)]
        a = [[None] * SB_HEADS for _ in blocks]
        for h in range(SB_HEADS):
            c = c_ref[h]
            for i, (_, vis) in enumerate(blocks):
                sp, s2 = terms[i][h]
                ah = jnp.exp(z[i][h] - sp - s2[:, :tk] - c)
                a[i][h] = (ah if vis is None else jnp.where(vis, ah, 0.0)).astype(BF16)
                c = c + s2[:, tk:]
            c_ref[h] = c
        for p, lanes in enumerate(pairs):
            upd = None
            for i, (ks, _) in enumerate(blocks):
                vv = _head_pair_rows(v_ref[0, pl.ds(ks, tk), lanes], head0)
                d = _dot(jnp.concatenate(a[i][2 * p:2 * p + 2], axis=1), vv)
                upd = d if upd is None else upd + d
            acc_ref[:, lanes] += upd

    key_blocks([(pl.multiple_of(qi * tq + d * tk, tk), (col + d * tk) < row) for d in range(n_diag - 1, -1, -1)])

    def full_body(jj, carry):
        last = qi * n_diag - 1 - jj * n_diag
        key_blocks([(pl.multiple_of((last - d) * tk, tk), None) for d in range(n_diag)])
        return carry

    lax.fori_loop(0, qi, full_body, 0)
    o_ref[0] = acc_ref[...]


def _suffix_matrix():
    j = np.arange(2 * SB_TK)[:, None] % SB_TK
    c = np.arange(2 * SB_TK)[None, :]
    return jnp.asarray(np.where(c < SB_TK, j > c, True), BF16)


def sb_prompt(p3, bias, *, tq):
    b, l, _ = p3.shape
    return pl.pallas_call(
        functools.partial(_sb_prompt_kernel, tq=tq),
        grid=(b, l // tq),
        in_specs=[pl.BlockSpec(memory_space=pltpu.SMEM),
                  pl.BlockSpec((1, tq, SB_W), lambda i, q: (i, q, P_SBQ // SB_W)),
                  pl.BlockSpec((1, l, SB_W), lambda i, q: (i, 0, P_SBK // SB_W)),
                  pl.BlockSpec((1, l, SB_W), lambda i, q: (i, 0, P_SBV // SB_W)),
                  pl.BlockSpec((2 * SB_TK, 2 * SB_TK), lambda i, q: (0, 0))],
        out_specs=pl.BlockSpec((1, tq, SB_W), lambda i, q: (i, q, 0)),
        out_shape=jax.ShapeDtypeStruct((b, l, SB_W), F32),
        scratch_shapes=[pltpu.VMEM((tq, SB_W), BF16), pltpu.VMEM((tq, SB_W), F32),
                        pltpu.VMEM((SB_HEADS, tq, SB_TK), F32)],
        compiler_params=_cparams(("parallel", "arbitrary")),
        name="sb_prompt",
    )(bias, p3, p3, p3, _suffix_matrix())


def _sb_own_head():
    return (lax.broadcasted_iota(jnp.int32, (SB_HEADS, SB_W), 0)
            == lax.broadcasted_iota(jnp.int32, (SB_HEADS, SB_W), 1) // SB_HEAD_DIM)


def _sb_decode_stages(q_ref, bias_ref, u2_ref, k_refs, v_refs, carry, acc, *, per_stage=4):
    nh = SB_HEADS
    group = len(k_refs)
    q = jnp.broadcast_to(q_ref[0] * (SB_HEAD_DIM ** -0.5), (nh, SB_W))
    qbd = jnp.where(_sb_own_head(), q, 0.0).astype(BF16)
    zs = []
    for g in range(group):
        zs.append(_dot(qbd, k_refs[g][...].astype(BF16)) + bias_ref[...])
        if g % per_stage == per_stage - 1:
            yield
    z = jnp.concatenate(zs, axis=0)
    sp, s2 = _sb_log_terms(z, None, u2_ref[...])
    yield
    later = [None] * group
    for g in range(group - 1, -1, -1):
        later[g] = carry
        carry = carry + s2[g * nh:(g + 1) * nh, SB_TK:]
    a = jnp.exp(z - sp - s2[:, :SB_TK] - jnp.concatenate(later, axis=0))
    yield
    for g in range(group):
        acc = acc + _dot_nt(a[g * nh:(g + 1) * nh].astype(BF16), v_refs[g][...].astype(BF16))
        if g % per_stage == per_stage - 1:
            yield
    return carry, acc


def _sb_decode_kernel(pt_ref, q_ref, bias_ref, u2_ref, *refs, group):
    k_refs, v_refs = refs[:group], refs[group:2 * group]
    o_ref, c_ref, acc_ref = refs[2 * group:]
    step = pl.program_id(1)

    @pl.when(step == 0)
    def _():
        c_ref[...] = jnp.zeros_like(c_ref)
        acc_ref[...] = jnp.zeros_like(acc_ref)

    (carry, acc), = _alternate(_sb_decode_stages(q_ref, bias_ref, u2_ref, k_refs, v_refs, c_ref[...], acc_ref[...]))
    c_ref[...] = carry
    acc_ref[...] = acc

    @pl.when(step == pl.num_programs(1) - 1)
    def _():
        o_ref[0] = jnp.sum(jnp.where(_sb_own_head(), acc, 0.0), axis=0, keepdims=True)


def sb_decode(q, cache_kt, cache_vt, page_table, bias, layer, *, group):
    bs, n_pages = page_table.shape
    page = cache_kt.shape[3]
    assert page == SB_TK and n_pages % group == 0
    n_steps = n_pages // group

    def page_spec(g):
        def index(b, s, pt):
            return (layer, pt[b * n_pages + (n_steps - 1 - s) * group + g], 0, 0)
        return pl.BlockSpec((None, None, SB_W, page), index)

    grid_spec = pltpu.PrefetchScalarGridSpec(
        num_scalar_prefetch=1,
        grid=(bs, n_steps),
        in_specs=[pl.BlockSpec((1, 1, SB_W), lambda b, s, pt: (b, 0, 0)),
                  pl.BlockSpec((SB_HEADS, 1), lambda b, s, pt: (0, 0)),
                  pl.BlockSpec((2 * SB_TK, 2 * SB_TK), lambda b, s, pt: (0, 0))]
        + [page_spec(g) for g in range(group)] * 2,
        out_specs=pl.BlockSpec((1, 1, SB_W), lambda b, s, pt: (b, 0, 0)),
        scratch_shapes=[pltpu.VMEM((SB_HEADS, SB_TK), F32), pltpu.VMEM((SB_HEADS, SB_W), F32)],
    )
    out = pl.pallas_call(
        functools.partial(_sb_decode_kernel, group=group),
        grid_spec=grid_spec,
        out_shape=jax.ShapeDtypeStruct((bs, 1, SB_W), F32),
        compiler_params=_cparams(("parallel", "arbitrary")),
        name="sb_decode",
    )(page_table.reshape(-1), q.reshape(bs, 1, SB_W), bias.reshape(SB_HEADS, 1), _suffix_matrix(),
      *([cache_kt] * group), *([cache_vt] * group))
    return out.reshape(bs, SB_W)


def _ffn_decode_kernel(pt_ref, x_ref, g_ref, wi_ref, wo_ref, gf_ref, q_ref, bias_ref, u2_ref, *refs, n_pages, final, tf):
    k_refs, v_refs = refs[:n_pages], refs[n_pages:2 * n_pages]
    o_ref, od_ref, act_ref = refs[2 * n_pages:]
    zero = lambda w: jnp.zeros((SB_HEADS, w), F32)
    _, (_, acc) = _alternate(
        _ffn_stages(x_ref, g_ref, wi_ref, wo_ref, gf_ref, o_ref, act_ref, final=final, tf=tf),
        _sb_decode_stages(q_ref, bias_ref, u2_ref, k_refs, v_refs, zero(SB_TK), zero(SB_W), per_stage=2))
    od_ref[0] = jnp.sum(jnp.where(_sb_own_head(), acc, 0.0), axis=0, keepdims=True)


def ffn_decode(x, g, w_in, w_out, gf, q, cache_kt, cache_vt, page_table, bias, layer, *, tf, final):
    t, d = x.shape
    dff = w_out.shape[1]
    bs, n_pages = page_table.shape
    page = cache_kt.shape[3]
    tm = t // bs
    assert page == SB_TK and tm * bs == t and tm % 8 == 0

    def page_spec(g):
        return pl.BlockSpec((None, None, SB_W, page), lambda i, pt: (layer, pt[i * n_pages + g], 0, 0))

    grid_spec = pltpu.PrefetchScalarGridSpec(
        num_scalar_prefetch=1,
        grid=(bs,),
        in_specs=_ffn_in_specs(tm, d, dff, layer)
        + [pl.BlockSpec((1, 1, SB_W), lambda i, pt: (i, 0, 0)), _resident((SB_HEADS, 1)),
           _resident((2 * SB_TK, 2 * SB_TK))]
        + [page_spec(g) for g in range(n_pages)] * 2,
        out_specs=[pl.BlockSpec((tm, d), lambda i, pt: (i, 0)), pl.BlockSpec((1, 1, SB_W), lambda i, pt: (i, 0, 0))],
        scratch_shapes=[pltpu.VMEM((tm, dff), BF16)],
    )
    y, o = pl.pallas_call(
        functools.partial(_ffn_decode_kernel, n_pages=n_pages, final=final, tf=tf),
        grid_spec=grid_spec,
        out_shape=[jax.ShapeDtypeStruct((t, d), F32), jax.ShapeDtypeStruct((bs, 1, SB_W), F32)],
        compiler_params=_cparams(("parallel",)),
        name="ffn_decode",
    )(page_table.reshape(-1), x, g.reshape(1, d), w_in, w_out, gf.reshape(1, d), q.reshape(bs, 1, SB_W),
      bias.reshape(SB_HEADS, 1), _suffix_matrix(), *([cache_kt] * n_pages), *([cache_vt] * n_pages))
    return y, o.reshape(bs, SB_W)


def _head_indicator(width, head):
    i = np.arange(width) // head
    return jnp.asarray(i[:, None] == i[None, :], BF16)


def _gdn_qkv_post(conv_out, seg):
    act = conv_out * _sigmoid(conv_out)
    q = act[:, :GDN_QK_W]
    k = act[:, GDN_QK_W:2 * GDN_QK_W]
    v = act[:, 2 * GDN_QK_W:]
    qn = q * lax.rsqrt(_dot_exact_rhs(q * q, seg) + L2_EPS) * (GDN_DK ** -0.5)
    kn = k * lax.rsqrt(_dot_exact_rhs(k * k, seg) + L2_EPS)
    return qn, kn, v


def _gdn_gates(ab, alog, dtb):
    g = -jnp.exp(alog) * _softplus(ab + dtb)
    return g, _sigmoid(ab)


def _gdn_pre_kernel(x_ref, halo_ref, ab_ref, cw_ref, alog_ref, dtb_ref, seg_ref, q_ref, k_ref, v_ref, gb_ref):
    i = pl.program_id(1)
    x = x_ref[0]
    tl = x.shape[0]
    halo = jnp.where(i > 0, halo_ref[0], 0.0)
    nh = halo.shape[0]
    ext = jnp.concatenate([halo, x], axis=0)
    out = None
    for tap in range(GDN_CONV):
        d = GDN_CONV - 1 - tap
        src = x if d == 0 else pltpu.roll(ext, d, axis=0)[nh:]
        term = src * cw_ref[tap:tap + 1, :]
        out = term if out is None else out + term
    qn, kn, v = _gdn_qkv_post(out, seg_ref[...])
    q_ref[0] = qn
    k_ref[0] = kn
    v_ref[0] = v

    g, beta = _gdn_gates(ab_ref[0], alog_ref[...], dtb_ref[...])
    tt = lax.broadcasted_iota(jnp.int32, g.shape, 0) % GDN_CHUNK
    k = 1
    while k < GDN_CHUNK:
        g = g + jnp.where(tt >= k, pltpu.roll(g, k, axis=0), 0.0)
        k *= 2
    lane = lax.broadcasted_iota(jnp.int32, g.shape, 1)
    gb_ref[0] = jnp.where(lane < GDN_HEADS, g, beta)


def gdn_pre(p3, conv_w, alog, dtb, *, tl):
    b, l, _ = p3.shape
    nh = 8
    out_sd = jax.ShapeDtypeStruct((b, l, GDN_QK_W), F32)
    return pl.pallas_call(
        _gdn_pre_kernel,
        grid=(b, l // tl),
        in_specs=[pl.BlockSpec((1, tl, GDN_QKV_W), lambda n, i: (n, i, P_GQKV // GDN_QKV_W)),
                  pl.BlockSpec((1, nh, GDN_QKV_W), lambda n, i: (n, jnp.maximum(i * (tl // nh) - 1, 0), 0)),
                  pl.BlockSpec((1, tl, LANES), lambda n, i: (n, i, P_AB // LANES)),
                  pl.BlockSpec((GDN_CONV, GDN_QKV_W), lambda n, i: (0, 0)),
                  pl.BlockSpec((1, LANES), lambda n, i: (0, 0)),
                  pl.BlockSpec((1, LANES), lambda n, i: (0, 0)),
                  pl.BlockSpec((GDN_QK_W, GDN_QK_W), lambda n, i: (0, 0))],
        out_specs=[pl.BlockSpec((1, tl, GDN_QK_W), lambda n, i: (n, i, 0))] * 3
        + [pl.BlockSpec((1, tl, LANES), lambda n, i: (n, i, 0))],
        out_shape=[out_sd, out_sd, out_sd, jax.ShapeDtypeStruct((b, l, LANES), F32)],
        compiler_params=_cparams(("parallel", "parallel")),
        name="gdn_pre",
    )(p3, p3, p3, conv_w, alog, dtb, _head_indicator(GDN_QK_W, GDN_DK))


def _gate_broadcast_matrix():
    r = np.arange(2 * LANES)[:, None] % LANES
    c = np.arange(2 * LANES)[None, :]
    return jnp.asarray(np.where(c < LANES, r < GDN_HEADS, (r >= GDN_HEADS) & (r < 2 * GDN_HEADS)), BF16)


def _gdn_chunk_terms(q, k, v, gb, e_mat):
    ch = GDN_CHUNK
    r = GDN_HEADS * ch
    sh = ch.bit_length() - 1
    chunks = range(len(q))
    ri = lax.broadcasted_iota(jnp.int32, (r, r), 0)
    ci = lax.broadcasted_iota(jnp.int32, (r, r), 1)
    same_head = (ri >> sh) == (ci >> sh)
    incl = same_head & (ri >= ci)
    strict = same_head & (ri > ci)
    eye = (ri == ci).astype(F32)

    def stack(x):
        return jnp.where(same_head, jnp.concatenate([x] * GDN_HEADS, axis=0), 0.0)

    def widen(x):
        return jnp.concatenate([x] * (r // LANES), axis=1)

    kbd, qbd, vbd = [stack(x) for x in k], [stack(x) for x in q], [stack(x) for x in v]
    yield

    l1 = lax.broadcasted_iota(jnp.int32, (r, LANES), 1)
    r1 = lax.broadcasted_iota(jnp.int32, (r, LANES), 0) >> sh
    gate_lane = (l1 == r1) | (l1 == r1 + GDN_HEADS)
    gsel = [_split(jnp.where(gate_lane, jnp.concatenate([x] * GDN_HEADS, axis=0), 0.0)) for x in gb]
    gbc = [_dot(jnp.concatenate([hi, lo], axis=1), e_mat) for hi, lo in gsel]
    g1 = [x[:, :LANES] for x in gbc]
    gl1 = [jnp.concatenate([jnp.broadcast_to(x[(h + 1) * ch - 1:(h + 1) * ch, :], (ch, LANES))
                            for h in range(GDN_HEADS)], axis=0) for x in g1]
    gcol = [widen(x) for x in g1]
    bcol = [widen(x[:, LANES:]) for x in gbc]
    eg = [widen(jnp.exp(x)) for x in g1]
    egl = [widen(jnp.exp(x)) for x in gl1]
    ekd = [widen(jnp.exp(gl1[i] - g1[i])) for i in chunks]

    yield
    decay = [jnp.where(incl, jnp.exp(jnp.where(incl, x - x.T, 0.0)), 0.0) for x in gcol]
    kb = [kbd[i] * bcol[i] for i in chunks]
    kbd_b = [x.astype(BF16) for x in kbd]
    a_mat = [jnp.where(strict, _dot_nt(kb[i].astype(BF16), kbd_b[i]) * decay[i], 0.0) for i in chunks]
    yield

    t_inv = [eye - jnp.where((ri >> 1) == (ci >> 1), x, 0.0) for x in a_mat]
    for lv in range(1, sh):
        off = ((ri >> (lv + 1)) == (ci >> (lv + 1))) & ((ri >> lv) != (ci >> lv))
        tb = [x.astype(BF16) for x in t_inv]
        m = [_dot(jnp.where(off, a_mat[i], 0.0).astype(BF16), tb[i]).astype(BF16) for i in chunks]
        yield
        t_inv = [t_inv[i] - _dot(tb[i], m[i]) for i in chunks]
        yield
    res = [(eye - t_inv[i]) - _mm3(a_mat[i], t_inv[i]) for i in chunks]
    yield
    t_inv = [t_inv[i] + _dot(t_inv[i].astype(BF16), res[i].astype(BF16)) for i in chunks]
    t_split = [_split(x) for x in t_inv]
    yield

    def apply_t(i, x):
        th, tl = t_split[i]
        xh, xl = _split(x)
        return _dot(th, xh) + (_dot(th, xl) + _dot(tl, xh))

    u = [apply_t(i, vbd[i] * bcol[i]) for i in chunks]
    yield
    w = [apply_t(i, kb[i] * eg[i]).astype(BF16) for i in chunks]
    yield
    qk = [jnp.where(incl, _dot_nt(qbd[i].astype(BF16), kbd_b[i]) * decay[i], 0.0).astype(BF16) for i in chunks]
    qd = [(qbd[i] * eg[i]).astype(BF16) for i in chunks]
    kd = [(kbd[i] * ekd[i]).astype(BF16) for i in chunks]
    return list(zip(u, w, qk, qd, kd, egl))


def _gdn_chunk_kernel(q_ref, k_ref, v_ref, gb_ref, z_ref, gn_ref, e_ref, seg_ref, og_ref, s_ref, st_ref, *, nc):
    c = pl.program_id(1)
    ch = GDN_CHUNK

    @pl.when(c == 0)
    def _():
        st_ref[...] = jnp.zeros_like(st_ref)

    def terms_of(chunk_ids):
        rows = [slice(i * ch, (i + 1) * ch) for i in chunk_ids]
        return _gdn_chunk_terms([q_ref[0, x, :] for x in rows], [k_ref[0, x, :] for x in rows],
                                [v_ref[0, x, :] for x in rows], [gb_ref[0, x, :] for x in rows], e_ref[...])

    state = {"s": st_ref[...]}
    outs = []

    def recurrence(terms):
        for u, w, qk, qd, kd, egl in terms:
            sb = state["s"].astype(BF16)
            vnb = (u - _dot(w, sb)).astype(BF16)
            o = _dot(qd, sb)
            yield
            o = o + _dot(qk, vnb)
            state["s"] = state["s"] * egl + _dot_tn(kd, vnb)
            oc = o[0:ch]
            for h in range(1, GDN_HEADS):
                oc = oc + o[h * ch:(h + 1) * ch]
            outs.append(oc)
            yield

    half = max(nc // 2, 1)
    terms_a, = _alternate(terms_of(range(half)))
    if nc > half:
        terms_b, _ = _alternate(terms_of(range(half, nc)), recurrence(terms_a))
        _alternate(recurrence(terms_b))
    else:
        _alternate(recurrence(terms_a))
    s = state["s"]
    st_ref[...] = s

    oc = jnp.concatenate(outs, axis=0)
    ms = _dot_exact_rhs(oc * oc, seg_ref[...]) * (1.0 / GDN_DV)
    zz = z_ref[0]
    og_ref[0] = oc * lax.rsqrt(ms + RMS_EPS) * gn_ref[...] * (zz * _sigmoid(zz))

    @pl.when(c == pl.num_programs(1) - 1)
    def _():
        for h in range(GDN_HEADS):
            s_ref[0, h] = s[h * GDN_DK:(h + 1) * GDN_DK, h * GDN_DV:(h + 1) * GDN_DV]


def gdn_chunk(qn, kn, v, gb, p3, gnorm, *, nc):
    b, l, _ = qn.shape
    ch = GDN_CHUNK * nc
    r = GDN_HEADS * GDN_CHUNK
    blk = lambda w: pl.BlockSpec((1, ch, w), lambda n, c: (n, c, 0))
    return pl.pallas_call(
        functools.partial(_gdn_chunk_kernel, nc=nc),
        grid=(b, l // ch),
        in_specs=[blk(GDN_QK_W), blk(GDN_QK_W), blk(GDN_V_W), blk(LANES),
                  pl.BlockSpec((1, ch, GDN_V_W), lambda n, c: (n, c, P_GZ // GDN_V_W)),
                  pl.BlockSpec((1, GDN_V_W), lambda n, c: (0, 0)),
                  pl.BlockSpec((2 * LANES, 2 * LANES), lambda n, c: (0, 0)),
                  pl.BlockSpec((GDN_V_W, GDN_V_W), lambda n, c: (0, 0))],
        out_specs=[pl.BlockSpec((1, ch, GDN_V_W), lambda n, c: (n, c, 0)),
                   pl.BlockSpec((1, GDN_HEADS, GDN_DK, GDN_DV), lambda n, c: (n, 0, 0, 0))],
        out_shape=[jax.ShapeDtypeStruct((b, l, GDN_V_W), F32),
                   jax.ShapeDtypeStruct((b, GDN_HEADS, GDN_DK, GDN_DV), F32)],
        scratch_shapes=[pltpu.VMEM((r, r), F32)],
        compiler_params=_cparams(("parallel", "arbitrary")),
        name="gdn_chunk",
    )(qn, kn, v, gb, p3, jnp.tile(gnorm, GDN_HEADS).reshape(1, GDN_V_W), _gate_broadcast_matrix(),
      _head_indicator(GDN_V_W, GDN_DV))


def _sample_pre_kernel(up_ref, x_ref, ab_ref, z_ref, ph_ref, ch_ref, pw_ref, psc_ref, cw_ref, alog_ref, dtb_ref,
                       seg_ref, *refs, pos0, carried):
    op_ref, qt_ref, kt_ref, vt_ref, zt_ref, gbt_ref, pn_ref, cn_ref = refs[2 * carried:]
    u = up_ref[...]
    hist = ph_ref.shape[0]
    grp = lax.broadcasted_iota(jnp.int32, u.shape, 1) // POOL_GC
    tails, tail = {0: jnp.zeros_like(u)}, jnp.zeros_like(u)
    for n in range(1, hist + 1):
        tail = tail + ph_ref[hist - n]
        tails[n] = tail
    means = [(tails[min(win - 1, hist)] + u) / float(min(win, pos0 + 1)) for win in POOL_WINDOWS]
    pooled = _pool_select(grp, means) - u
    op_ref[...] = _dot(pooled.astype(BF16), pw_ref[...]) * psc_ref[...]
    pn_ref[0:hist - 1] = ph_ref[1:hist]
    pn_ref[hist - 1] = u

    x = x_ref[...]
    out = ch_ref[0] * cw_ref[0:1, :]
    for tap in range(1, GDN_CONV - 1):
        out = out + ch_ref[tap] * cw_ref[tap:tap + 1, :]
    out = out + x * cw_ref[GDN_CONV - 1:, :]
    cn_ref[0:GDN_CONV - 2] = ch_ref[1:GDN_CONV - 1]
    cn_ref[GDN_CONV - 2] = x
    qn, kn, v = _gdn_qkv_post(out, seg_ref[...])
    qt_ref[...] = qn.T
    kt_ref[...] = kn.T
    vt_ref[...] = v.T
    zt_ref[...] = z_ref[...].T

    g, beta = _gdn_gates(ab_ref[...], alog_ref[...], dtb_ref[...])
    lane = lax.broadcasted_iota(jnp.int32, g.shape, 1)
    gbt_ref[...] = jnp.where(lane < GDN_HEADS, jnp.exp(g), beta).T


def sample_pre(p, pool_hist, conv_hist, pool_new, conv_new, layer, pool_w_bd, pool_scale, conv_w, alog, dtb, *, pos0):
    bs = p.shape[0]
    depth = pool_hist.shape[0]
    full = lambda shape: pl.BlockSpec(shape, lambda i: (0,) * len(shape))
    of_layer = lambda shape: pl.BlockSpec((None, *shape), lambda i: (layer,) + (0,) * len(shape))
    sd = lambda *shape: jax.ShapeDtypeStruct(shape, F32)
    pool_rows, conv_rows = pool_hist.shape[1:], conv_hist.shape[1:]
    carried = pool_new is not None
    outs = pl.pallas_call(
        functools.partial(_sample_pre_kernel, pos0=pos0, carried=carried),
        grid=(1,),
        in_specs=[pl.BlockSpec((bs, POOL_W), lambda i: (0, P_POOL // POOL_W)),
                  pl.BlockSpec((bs, GDN_QKV_W), lambda i: (0, P_GQKV // GDN_QKV_W)),
                  pl.BlockSpec((bs, LANES), lambda i: (0, P_AB // LANES)),
                  pl.BlockSpec((bs, GDN_V_W), lambda i: (0, P_GZ // GDN_V_W)),
                  of_layer(pool_rows), of_layer(conv_rows), of_layer((POOL_W, POOL_W)), full((1, POOL_W)),
                  full((GDN_CONV, GDN_QKV_W)), full((1, LANES)), full((1, LANES)), full((GDN_QK_W, GDN_QK_W))]
        + [pl.BlockSpec(memory_space=pl.ANY)] * (2 * carried),
        out_specs=[full((bs, POOL_W)), full((GDN_QK_W, bs)), full((GDN_QK_W, bs)), full((GDN_V_W, bs)),
                   full((GDN_V_W, bs)), full((LANES, bs)), of_layer(pool_rows), of_layer(conv_rows)],
        out_shape=[sd(bs, POOL_W), sd(GDN_QK_W, bs), sd(GDN_QK_W, bs), sd(GDN_V_W, bs), sd(GDN_V_W, bs),
                   sd(LANES, bs), sd(depth, *pool_rows), sd(depth, *conv_rows)],
        input_output_aliases={12: 6, 13: 7} if carried else {},
        compiler_params=_cparams(("arbitrary",)),
        name="sample_pre",
    )(p, p, p, p, pool_hist, conv_hist, pool_w_bd, pool_scale.reshape(1, POOL_W), conv_w, alog, dtb,
      _head_indicator(GDN_QK_W, GDN_DK), *([pool_new, conv_new] if carried else []))
    return outs


def _gdn_step_kernel(s_ref, k_ref, q_ref, v_ref, z_ref, gb_ref, gn_ref, *refs):
    so_ref, og_ref = refs[-2:]
    h = pl.program_id(0)
    eg = gb_ref[pl.ds(h, 1), :]
    beta = gb_ref[pl.ds(h + GDN_HEADS, 1), :]
    ks = None
    for dk in range(GDN_DK):
        term = k_ref[dk:dk + 1, :] * (s_ref[dk] * eg)
        ks = term if ks is None else ks + term
    delta = (v_ref[...] - ks) * beta
    o = None
    for dk in range(GDN_DK):
        s_new = s_ref[dk] * eg + k_ref[dk:dk + 1, :] * delta
        so_ref[dk] = s_new
        term = q_ref[dk:dk + 1, :] * s_new
        o = term if o is None else o + term
    ms = jnp.mean(o * o, axis=0, keepdims=True)
    zz = z_ref[...]
    og_ref[...] = o * lax.rsqrt(ms + RMS_EPS) * gn_ref[...] * (zz * _sigmoid(zz))


def gdn_step(state, state_new, layer, kt, qt, vt, zt, gbt, gnorm):
    depth, h, dk, dv, bs = state.shape
    of_head = lambda rows: pl.BlockSpec((rows, bs), lambda i: (i, 0))
    s_spec = pl.BlockSpec((None, None, dk, dv, bs), lambda i: (layer, i, 0, 0, 0))
    carried = state_new is not None
    return pl.pallas_call(
        _gdn_step_kernel,
        grid=(h,),
        in_specs=[s_spec, of_head(dk), of_head(dk), of_head(dv), of_head(dv),
                  pl.BlockSpec((LANES, bs), lambda i: (0, 0)), pl.BlockSpec((dv, 1), lambda i: (0, 0))]
        + [pl.BlockSpec(memory_space=pl.ANY)] * carried,
        out_specs=[s_spec, of_head(dv)],
        out_shape=[jax.ShapeDtypeStruct(state.shape, F32), jax.ShapeDtypeStruct((h * dv, bs), F32)],
        input_output_aliases={7: 0} if carried else {},
        compiler_params=_cparams(("parallel",)),
        name="gdn_step",
    )(state, kt, qt, vt, zt, gbt, gnorm.reshape(dv, 1), *([state_new] if carried else []))


def _largest_tile(n, cap):
    t = min(n, cap)
    while n % t:
        t //= 2
    return t


def _pack_w_in(w_in):
    depth, d, _ = w_in.shape
    pad = jnp.zeros((depth, d, P_W - P_AB - (R_GZ - R_GA)), w_in.dtype)
    return jnp.concatenate([w_in[..., R_GQKV:R_GA], w_in[..., R_POOL:R_GQKV], w_in[..., R_GZ:R_GATE],
                            w_in[..., R_GA:R_GZ], pad], axis=-1).astype(BF16)


def _lane_pad(x):
    return jnp.pad(x, ((0, 0), (0, LANES - x.shape[1])))


def kernel(x_prompt, x_sample, cache_sb_k, cache_sb_v, page_table, state_pool, state_gdn_conv, state_gdn, norm1_g, w_in, pool_w, pool_scale, sb_bias, gdn_conv_w, gdn_a_log, gdn_dt_bias, gdn_norm_g, w_branch, w_o, norm2_g, w_ffn_in, w_ffn_out, normf_g):
    bp, seq, d = x_prompt.shape
    bs = x_sample.shape[0]
    depth = w_in.shape[0]
    n_phys, page = cache_sb_k.shape[1], cache_sb_k.shape[2]
    past_len = page_table.shape[1] * page
    tp = bp * seq

    w_in_p = _pack_w_in(w_in)
    w_gate_b = w_in[..., R_GATE:R_END].astype(BF16)
    w_branch_b = w_branch.astype(BF16)
    w_o_b = w_o.astype(BF16)
    w_ffn_in_b = w_ffn_in.astype(BF16)
    w_ffn_out_b = w_ffn_out.astype(BF16)
    eye_g = jnp.eye(len(POOL_WINDOWS), dtype=F32)
    pool_w_bd = jnp.einsum("lgcd,gh->lgchd", pool_w, eye_g).reshape(depth, POOL_W, POOL_W).astype(BF16)
    alog_p = _lane_pad(gdn_a_log)
    dtb_p = _lane_pad(gdn_dt_bias)
    cache_kt = jnp.transpose(cache_sb_k, (0, 1, 3, 4, 2)).reshape(depth, n_phys, SB_W, page)
    cache_vt = jnp.transpose(cache_sb_v, (0, 1, 3, 4, 2)).reshape(depth, n_phys, SB_W, page)
    n_pages = page_table.shape[1]
    page_group = _largest_tile(n_pages, 16)
    fuse_decode = page_group == n_pages and tp % bs == 0 and (tp // bs) % 8 == 0

    tm_p = _largest_tile(tp, 512)
    tm_merge = _largest_tile(tp, 512)
    tm_s = _largest_tile(bs, 128)
    tq = _largest_tile(seq, 256)
    tl = _largest_tile(seq, 256)

    xp = x_prompt.reshape(tp, d)
    xs = x_sample.reshape(bs, d)
    pool_hist = jnp.transpose(state_pool, (0, 2, 1, 3))
    conv_hist = jnp.transpose(state_gdn_conv, (0, 2, 1, 3))
    gdn_state = jnp.transpose(state_gdn, (0, 2, 3, 4, 1))
    pool_s = conv_s = gdn_s = None

    pool_p, conv_p, gdn_p, k_s, v_s = [], [], [], [], []
    kt_new = vt_new = None
    for l in range(depth):
        final = l == depth - 1
        p, kt_new, vt_new = rms_matmul(xp, norm1_g[l], w_in_p, l, tm=tm_p, tn=1024,
                                       kv_t=(depth, seq, kt_new, vt_new))
        p3 = p.reshape(bp, seq, P_W)
        o_pool = pool_prompt(p3, pool_w_bd[l], pool_scale[l])
        o_sb = sb_prompt(p3, sb_bias[l], tq=tq)
        qn, kn, gv, gb = gdn_pre(p3, gdn_conv_w[l], alog_p[l:l + 1], dtb_p[l:l + 1], tl=tl)
        o_g, s_p = gdn_chunk(qn, kn, gv, gb, p3, gdn_norm_g[l], nc=_largest_tile(seq // GDN_CHUNK, 8))
        xp = merge(xp, norm1_g[l], o_pool.reshape(tp, POOL_W), o_sb.reshape(tp, SB_W), o_g.reshape(tp, GDN_V_W),
                   w_gate_b, w_branch_b, w_o_b, l, tm=tm_merge)
        ps = rms_matmul(xs, norm1_g[l], w_in_p, l, tm=tm_s, tn=1024)
        if fuse_decode:
            xp, o_sb_s = ffn_decode(xp, norm2_g[l], w_ffn_in_b, w_ffn_out_b, normf_g, ps[:, P_SBQ:P_SBK],
                                    cache_kt, cache_vt, page_table, sb_bias[l], l, tf=256, final=final)
        else:
            xp = ffn(xp, norm2_g[l], w_ffn_in_b, w_ffn_out_b, normf_g, l, tm=tm_p, tf=256, final=final)
            o_sb_s = sb_decode(ps[:, P_SBQ:P_SBK], cache_kt, cache_vt, page_table, sb_bias[l], l, group=page_group)
        pool_p.append(p3[:, seq - POOL_HIST:, P_POOL:P_SBQ])
        conv_p.append(p3[:, seq - (GDN_CONV - 1):, P_GQKV:P_POOL])
        gdn_p.append(s_p)

        o_pool_s, qt_s, kt_s, vt_s, zt_s, gbt_s, pool_s, conv_s = sample_pre(
            ps, pool_hist, conv_hist, pool_s, conv_s, l, pool_w_bd, pool_scale[l], gdn_conv_w[l], alog_p[l:l + 1],
            dtb_p[l:l + 1], pos0=past_len)
        gdn_s, ogt_s = gdn_step(gdn_state, gdn_s, l, kt_s, qt_s, vt_s, zt_s, gbt_s, gdn_norm_g[l])
        xs = merge(xs, norm1_g[l], o_pool_s, o_sb_s, ogt_s.T, w_gate_b, w_branch_b, w_o_b, l, tm=tm_s)
        xs = ffn(xs, norm2_g[l], w_ffn_in_b, w_ffn_out_b, normf_g, l, tm=tm_s, tf=256, final=final)
        k_s.append(ps[:, P_SBK:P_SBV].reshape(bs, 1, SB_HEADS, SB_HEAD_DIM))
        v_s.append(ps[:, P_SBV:P_GZ].reshape(bs, 1, SB_HEADS, SB_HEAD_DIM))

    y_prompt = xp.reshape(bp, seq, d)
    y_sample = xs.reshape(bs, 1, d)
    new_kv = [jnp.transpose(t.reshape(depth, bp, SB_HEADS, SB_HEAD_DIM, seq), (0, 1, 4, 2, 3))
              for t in (kt_new, vt_new)]
    return (y_prompt, y_sample, *new_kv, jnp.stack(pool_p), jnp.stack(conv_p), jnp.stack(gdn_p),
            jnp.stack(k_s), jnp.stack(v_s), jnp.transpose(pool_s, (0, 2, 1, 3)), jnp.transpose(conv_s, (0, 2, 1, 3)),
            jnp.transpose(gdn_s, (0, 4, 1, 2, 3)))
```

```python
import functools

import numpy as np
import jax
import jax.numpy as jnp
from jax import lax
from jax.experimental import pallas as pl
from jax.experimental.pallas import tpu as pltpu

F32 = jnp.float32
BF16 = jnp.bfloat16

D_MODEL = 1024
POOL_WINDOWS = (2, 4, 8, 16)
POOL_W = 256
POOL_GC = 64
POOL_HIST = 15
SB_HEADS = 8
SB_HEAD_DIM = 64
SB_W = 512
GDN_HEADS = 4
GDN_DK = 64
GDN_DV = 64
GDN_QK_W = 256
GDN_V_W = 256
GDN_QKV_W = 768
GDN_CONV = 4
GDN_CHUNK = 64
D_FF = 2816
RMS_EPS = 1e-6
L2_EPS = 1e-6

R_POOL, R_SBQ, R_GQKV, R_GA, R_GZ, R_GATE, R_END = 0, 256, 1792, 2560, 2568, 2824, 5896
P_GQKV, P_POOL, P_SBQ, P_SBK, P_SBV, P_GZ, P_AB, P_W = 0, 768, 1024, 1536, 2048, 2560, 2816, 3072

LANES = 128
SB_TK = LANES
VMEM_LIMIT = 56 * 1024 * 1024


def _cparams(sem):
    return pltpu.CompilerParams(dimension_semantics=sem, vmem_limit_bytes=VMEM_LIMIT)


def _sigmoid(x):
    return 1.0 / (1.0 + jnp.exp(-x))


NEG_LOG2E = -1.4426950408889634


def _softplus(x):
    return jnp.maximum(x, 0.0) + jnp.log(1.0 + jnp.exp2(jnp.abs(x) * NEG_LOG2E))


def _split(x):
    hi = x.astype(BF16)
    lo = (x - hi.astype(F32)).astype(BF16)
    return hi, lo


def _dot(a, b):
    return jnp.dot(a, b, preferred_element_type=F32)


def _dot_nt(a, b):
    return lax.dot_general(a, b, (((1,), (1,)), ((), ())), preferred_element_type=F32)


def _dot_tn(a, b):
    return lax.dot_general(a, b, (((0,), (0,)), ((), ())), preferred_element_type=F32)


def _mm3(a, b):
    ah, al = _split(a)
    bh, bl = _split(b)
    return _dot(ah, bh) + (_dot(ah, bl) + _dot(al, bh))


def _dot_exact_rhs(x, m):
    hi, lo = _split(x)
    return _dot(hi, m) + _dot(lo, m)


def _rms_scale(x):
    return lax.rsqrt(jnp.mean(x * x, axis=-1, keepdims=True) + RMS_EPS)


def _alternate(*stage_gens):
    results = [None] * len(stage_gens)
    live = list(range(len(stage_gens)))
    while live:
        for i in list(live):
            try:
                next(stage_gens[i])
            except StopIteration as done:
                results[i] = done.value
                live.remove(i)
    return results


def _resident(shape, layer=None):
    if layer is None:
        return pl.BlockSpec(shape, lambda *_: (0,) * len(shape), pipeline_mode=pl.Buffered(1))
    return pl.BlockSpec((None, *shape), lambda *_: (layer,) + (0,) * len(shape), pipeline_mode=pl.Buffered(1))


def _rms_matmul_kernel(x_ref, g_ref, w_ref, *refs, tn, kv_t):
    o_ref = refs[-3] if kv_t else refs[-1]
    x = x_ref[...]
    h = (x * _rms_scale(x) * g_ref[...]).astype(BF16)
    for j in range(w_ref.shape[1] // tn):
        o_ref[:, j * tn:(j + 1) * tn] = _dot(h, w_ref[:, j * tn:(j + 1) * tn])
    if kv_t:
        kt_ref, vt_ref = refs[-2:]
        kt_ref[0] = o_ref[:, P_SBK:P_SBV].T
        vt_ref[0] = o_ref[:, P_SBV:P_GZ].T


def rms_matmul(x, g, w, layer, *, tm, tn, kv_t=None):
    t, d = x.shape
    n = w.shape[2]
    in_specs = [pl.BlockSpec((tm, d), lambda i: (i, 0)), _resident((1, d)), _resident((d, n), layer)]
    out_specs = [pl.BlockSpec((tm, n), lambda i: (i, 0))]
    out_shape = [jax.ShapeDtypeStruct((t, n), F32)]
    operands = [x, g.reshape(1, d), w]
    aliases = {}
    if kv_t:
        depth, seq, k_buf, v_buf = kv_t
        tiles = seq // tm
        t_spec = pl.BlockSpec((None, 1, SB_W, tm), lambda i: (layer, i // tiles, 0, i % tiles))
        out_specs += [t_spec, t_spec]
        out_shape += [jax.ShapeDtypeStruct((depth, t // seq, SB_W, seq), F32)] * 2
        if k_buf is not None:
            in_specs += [pl.BlockSpec(memory_space=pl.ANY)] * 2
            operands += [k_buf, v_buf]
            aliases = {3: 1, 4: 2}
    out = pl.pallas_call(
        functools.partial(_rms_matmul_kernel, tn=tn, kv_t=bool(kv_t)),
        grid=(t // tm,),
        in_specs=in_specs,
        out_specs=out_specs,
        out_shape=out_shape,
        input_output_aliases=aliases,
        compiler_params=_cparams(("parallel",)),
        name="in_proj",
    )(*operands)
    return out if kv_t else out[0]


def _ffn_stages(x_ref, g_ref, wi_ref, wo_ref, gf_ref, o_ref, act_ref, *, final, tf):
    x = x_ref[...]
    h = (x * _rms_scale(x) * g_ref[...]).astype(BF16)
    dff = wo_ref.shape[0]
    for f in range(dff // tf):
        gate = _dot(h, wi_ref[:, f * tf:(f + 1) * tf])
        up = _dot(h, wi_ref[:, dff + f * tf:dff + (f + 1) * tf])
        act_ref[:, f * tf:(f + 1) * tf] = ((gate * _sigmoid(gate)) * up).astype(BF16)
        yield
    y = x + _dot(act_ref[...], wo_ref[...])
    if final:
        y = y * _rms_scale(y) * gf_ref[...]
    o_ref[...] = y


def _ffn_kernel(*refs, final, tf):
    _alternate(_ffn_stages(*refs, final=final, tf=tf))


def _ffn_in_specs(tm, d, dff, layer):
    return [pl.BlockSpec((tm, d), lambda i, *_: (i, 0)), _resident((1, d)), _resident((d, 2 * dff), layer),
            _resident((dff, d), layer), _resident((1, d))]


def ffn(x, g, w_in, w_out, gf, layer, *, tm, tf, final):
    t, d = x.shape
    dff = w_out.shape[1]
    return pl.pallas_call(
        functools.partial(_ffn_kernel, final=final, tf=tf),
        grid=(t // tm,),
        in_specs=_ffn_in_specs(tm, d, dff, layer),
        out_specs=pl.BlockSpec((tm, d), lambda i: (i, 0)),
        out_shape=jax.ShapeDtypeStruct((t, d), F32),
        scratch_shapes=[pltpu.VMEM((tm, dff), BF16)],
        compiler_params=_cparams(("parallel",)),
        name="ffn",
    )(x, g.reshape(1, d), w_in, w_out, gf.reshape(1, d))


def _merge_kernel(x_ref, g_ref, op_ref, osb_ref, og_ref, wg_ref, wb_ref, wo_ref, o_ref):
    x = x_ref[...]
    d = x.shape[1]
    h = (x * _rms_scale(x) * g_ref[...]).astype(BF16)
    m = None
    row = 0
    for i, br_ref in enumerate((op_ref, osb_ref, og_ref)):
        width = br_ref.shape[1]
        branch = _dot(br_ref[...].astype(BF16), wb_ref[row:row + width, :])
        term = _sigmoid(_dot(h, wg_ref[:, i * d:(i + 1) * d])) * branch
        m = term if m is None else m + term
        row += width
    o_ref[...] = x + _dot(m.astype(BF16), wo_ref[...])


def merge(x, g, o_pool, o_sb, o_g, w_gate, w_branch, w_o, layer, *, tm):
    t, d = x.shape
    return pl.pallas_call(
        _merge_kernel,
        grid=(t // tm,),
        in_specs=[pl.BlockSpec((tm, d), lambda i: (i, 0)), _resident((1, d)),
                  pl.BlockSpec((tm, POOL_W), lambda i: (i, 0)),
                  pl.BlockSpec((tm, SB_W), lambda i: (i, 0)),
                  pl.BlockSpec((tm, GDN_V_W), lambda i: (i, 0)),
                  _resident((d, 3 * d), layer), _resident((d, d), layer), _resident((d, d), layer)],
        out_specs=pl.BlockSpec((tm, d), lambda i: (i, 0)),
        out_shape=jax.ShapeDtypeStruct((t, d), F32),
        compiler_params=_cparams(("parallel",)),
        name="merge",
    )(x, g.reshape(1, d), o_pool, o_sb, o_g, w_gate, w_branch, w_o)


def _pool_select(lane_grp, vals):
    out = vals[-1]
    for gi in range(len(vals) - 2, -1, -1):
        out = jnp.where(lane_grp == gi, vals[gi], out)
    return out


def _pool_prompt_kernel(u_ref, w_ref, sc_ref, o_ref):
    u = u_ref[0]
    t = lax.broadcasted_iota(jnp.int32, u.shape, 0)
    grp = lax.broadcasted_iota(jnp.int32, u.shape, 1) // POOL_GC

    def shifted(x, k):
        return jnp.where(t >= k, pltpu.roll(x, k, axis=0), 0.0)

    sums = []
    s, w = u, 1
    for win in POOL_WINDOWS:
        while w < win:
            s = s + shifted(s, w)
            w *= 2
        sums.append(s)
    sel = _pool_select(grp, sums)
    win = _pool_select(grp, [jnp.full(u.shape, wn, jnp.int32) for wn in POOL_WINDOWS])
    cnt = jnp.minimum(win, t + 1).astype(F32)
    pooled = sel / cnt - u
    o_ref[0] = _dot(pooled.astype(BF16), w_ref[...]) * sc_ref[...]


def pool_prompt(p3, w_bd, scale):
    b, l, _ = p3.shape
    return pl.pallas_call(
        _pool_prompt_kernel,
        grid=(b,),
        in_specs=[pl.BlockSpec((1, l, POOL_W), lambda i: (i, 0, P_POOL // POOL_W)),
                  pl.BlockSpec((POOL_W, POOL_W), lambda i: (0, 0)),
                  pl.BlockSpec((1, POOL_W), lambda i: (0, 0))],
        out_specs=pl.BlockSpec((1, l, POOL_W), lambda i: (i, 0, 0)),
        out_shape=jax.ShapeDtypeStruct((b, l, POOL_W), F32),
        compiler_params=_cparams(("parallel",)),
        name="pool_prompt",
    )(p3, w_bd, scale.reshape(1, POOL_W))


def _sb_log_terms(z, vis, suffix_mat):
    sp = _softplus(z)
    hi, lo = _split(sp if vis is None else jnp.where(vis, sp, 0.0))
    return sp, _dot(jnp.concatenate([hi, lo], axis=1), suffix_mat)


def _head_pair_rows(x, first_head_lanes):
    return jnp.concatenate([jnp.where(first_head_lanes, x, 0.0), jnp.where(first_head_lanes, 0.0, x)],
                           axis=0).astype(BF16)


def _sb_prompt_kernel(bias_ref, q_ref, k_ref, v_ref, u2_ref, o_ref, qb_ref, acc_ref, c_ref, *, tq):
    qi = pl.program_id(1)
    tk = SB_TK
    hw = 2 * SB_HEAD_DIM
    npair = SB_HEADS // 2
    n_diag = tq // tk
    head0 = lax.broadcasted_iota(jnp.int32, (tk, hw), 1) < SB_HEAD_DIM
    row = lax.broadcasted_iota(jnp.int32, (tq, tk), 0)
    col = lax.broadcasted_iota(jnp.int32, (tq, tk), 1)
    qb_ref[...] = (q_ref[0] * (SB_HEAD_DIM ** -0.5)).astype(BF16)
    acc_ref[...] = jnp.zeros_like(acc_ref)
    c_ref[...] = jnp.zeros_like(c_ref)
    suffix_mat = u2_ref[...]

    pairs = [slice(p * hw, (p + 1) * hw) for p in range(npair)]

    def key_blocks(blocks):
        z2 = [[_dot_nt(qb_ref[r0:, lanes], _head_pair_rows(k_ref[0, pl.ds(ks, tk), lanes], head0)) for lanes in pairs]
              for ks, _, r0 in blocks]
        z = [[zb[h // 2][:, (h % 2) * tk:(h % 2 + 1) * tk] + bias_ref[h] for h in range(SB_HEADS)] for zb in z2]
        terms = [[_sb_log_terms(zh, vis, suffix_mat) for zh in zb] for zb, (_, vis, _) in zip(z, blocks)]
        a = [[None] * SB_HEADS for _ in blocks]
        for h in range(SB_HEADS):
            c = c_ref[h]
            for i, (_, vis, r0) in enumerate(blocks):
                sp, s2 = terms[i][h]
                ah = jnp.exp(z[i][h] - sp - s2[:, :tk] - c[r0:])
                a[i][h] = (ah if vis is None else jnp.where(vis, ah, 0.0)).astype(BF16)
                grown = c[r0:] + s2[:, tk:]
                c = grown if r0 == 0 else jnp.concatenate([c[:r0], grown], axis=0)
            c_ref[h] = c
        for p, lanes in enumerate(pairs):
            upd = {}
            for i, (ks, _, r0) in enumerate(blocks):
                vv = _head_pair_rows(v_ref[0, pl.ds(ks, tk), lanes], head0)
                d = _dot(jnp.concatenate(a[i][2 * p:2 * p + 2], axis=1), vv)
                upd[r0] = d if r0 not in upd else upd[r0] + d
            for r0, d in upd.items():
                acc_ref[r0:, lanes] += d

    key_blocks([(pl.multiple_of(qi * tq + d * tk, tk), ((col + d * tk) < row)[d * tk:], d * tk)
                for d in range(n_diag - 1, -1, -1)])

    def full_body(jj, carry):
        last = qi * n_diag - 1 - jj * n_diag
        key_blocks([(pl.multiple_of((last - d) * tk, tk), None, 0) for d in range(n_diag)])
        return carry

    lax.fori_loop(0, qi, full_body, 0)
    o_ref[0] = acc_ref[...]


def _suffix_matrix():
    j = np.arange(2 * SB_TK)[:, None] % SB_TK
    c = np.arange(2 * SB_TK)[None, :]
    return jnp.asarray(np.where(c < SB_TK, j > c, True), BF16)


def sb_prompt(p3, bias, *, tq):
    b, l, _ = p3.shape
    return pl.pallas_call(
        functools.partial(_sb_prompt_kernel, tq=tq),
        grid=(b, l // tq),
        in_specs=[pl.BlockSpec(memory_space=pltpu.SMEM),
                  pl.BlockSpec((1, tq, SB_W), lambda i, q: (i, q, P_SBQ // SB_W)),
                  pl.BlockSpec((1, l, SB_W), lambda i, q: (i, 0, P_SBK // SB_W)),
                  pl.BlockSpec((1, l, SB_W), lambda i, q: (i, 0, P_SBV // SB_W)),
                  pl.BlockSpec((2 * SB_TK, 2 * SB_TK), lambda i, q: (0, 0))],
        out_specs=pl.BlockSpec((1, tq, SB_W), lambda i, q: (i, q, 0)),
        out_shape=jax.ShapeDtypeStruct((b, l, SB_W), F32),
        scratch_shapes=[pltpu.VMEM((tq, SB_W), BF16), pltpu.VMEM((tq, SB_W), F32),
                        pltpu.VMEM((SB_HEADS, tq, SB_TK), F32)],
        compiler_params=_cparams(("parallel", "arbitrary")),
        name="sb_prompt",
    )(bias, p3, p3, p3, _suffix_matrix())


def _sb_own_head():
    return (lax.broadcasted_iota(jnp.int32, (SB_HEADS, SB_W), 0)
            == lax.broadcasted_iota(jnp.int32, (SB_HEADS, SB_W), 1) // SB_HEAD_DIM)


def _sb_decode_stages(q_ref, bias_ref, u2_ref, k_refs, v_refs, carry, acc, *, per_stage=4):
    nh = SB_HEADS
    group = len(k_refs)
    q = jnp.broadcast_to(q_ref[0] * (SB_HEAD_DIM ** -0.5), (nh, SB_W))
    qbd = jnp.where(_sb_own_head(), q, 0.0).astype(BF16)
    zs = []
    for g in range(group):
        zs.append(_dot(qbd, k_refs[g][...].astype(BF16)) + bias_ref[...])
        if g % per_stage == per_stage - 1:
            yield
    z = jnp.concatenate(zs, axis=0)
    sp, s2 = _sb_log_terms(z, None, u2_ref[...])
    yield
    later = [None] * group
    for g in range(group - 1, -1, -1):
        later[g] = carry
        carry = carry + s2[g * nh:(g + 1) * nh, SB_TK:]
    a = jnp.exp(z - sp - s2[:, :SB_TK] - jnp.concatenate(later, axis=0))
    yield
    for g in range(group):
        acc = acc + _dot_nt(a[g * nh:(g + 1) * nh].astype(BF16), v_refs[g][...].astype(BF16))
        if g % per_stage == per_stage - 1:
            yield
    return carry, acc


def _sb_decode_kernel(pt_ref, q_ref, bias_ref, u2_ref, *refs, group):
    k_refs, v_refs = refs[:group], refs[group:2 * group]
    o_ref, c_ref, acc_ref = refs[2 * group:]
    step = pl.program_id(1)

    @pl.when(step == 0)
    def _():
        c_ref[...] = jnp.zeros_like(c_ref)
        acc_ref[...] = jnp.zeros_like(acc_ref)

    (carry, acc), = _alternate(_sb_decode_stages(q_ref, bias_ref, u2_ref, k_refs, v_refs, c_ref[...], acc_ref[...]))
    c_ref[...] = carry
    acc_ref[...] = acc

    @pl.when(step == pl.num_programs(1) - 1)
    def _():
        o_ref[0] = jnp.sum(jnp.where(_sb_own_head(), acc, 0.0), axis=0, keepdims=True)


def sb_decode(q, cache_kt, cache_vt, page_table, bias, layer, *, group):
    bs, n_pages = page_table.shape
    page = cache_kt.shape[3]
    assert page == SB_TK and n_pages % group == 0
    n_steps = n_pages // group

    def page_spec(g):
        def index(b, s, pt):
            return (layer, pt[b * n_pages + (n_steps - 1 - s) * group + g], 0, 0)
        return pl.BlockSpec((None, None, SB_W, page), index)

    grid_spec = pltpu.PrefetchScalarGridSpec(
        num_scalar_prefetch=1,
        grid=(bs, n_steps),
        in_specs=[pl.BlockSpec((1, 1, SB_W), lambda b, s, pt: (b, 0, 0)),
                  pl.BlockSpec((SB_HEADS, 1), lambda b, s, pt: (0, 0)),
                  pl.BlockSpec((2 * SB_TK, 2 * SB_TK), lambda b, s, pt: (0, 0))]
        + [page_spec(g) for g in range(group)] * 2,
        out_specs=pl.BlockSpec((1, 1, SB_W), lambda b, s, pt: (b, 0, 0)),
        scratch_shapes=[pltpu.VMEM((SB_HEADS, SB_TK), F32), pltpu.VMEM((SB_HEADS, SB_W), F32)],
    )
    out = pl.pallas_call(
        functools.partial(_sb_decode_kernel, group=group),
        grid_spec=grid_spec,
        out_shape=jax.ShapeDtypeStruct((bs, 1, SB_W), F32),
        compiler_params=_cparams(("parallel", "arbitrary")),
        name="sb_decode",
    )(page_table.reshape(-1), q.reshape(bs, 1, SB_W), bias.reshape(SB_HEADS, 1), _suffix_matrix(),
      *([cache_kt] * group), *([cache_vt] * group))
    return out.reshape(bs, SB_W)


def _ffn_decode_kernel(pt_ref, x_ref, g_ref, wi_ref, wo_ref, gf_ref, q_ref, bias_ref, u2_ref, *refs, n_pages, final, tf):
    k_refs, v_refs = refs[:n_pages], refs[n_pages:2 * n_pages]
    o_ref, od_ref, act_ref = refs[2 * n_pages:]
    zero = lambda w: jnp.zeros((SB_HEADS, w), F32)
    _, (_, acc) = _alternate(
        _ffn_stages(x_ref, g_ref, wi_ref, wo_ref, gf_ref, o_ref, act_ref, final=final, tf=tf),
        _sb_decode_stages(q_ref, bias_ref, u2_ref, k_refs, v_refs, zero(SB_TK), zero(SB_W), per_stage=2))
    od_ref[0] = jnp.sum(jnp.where(_sb_own_head(), acc, 0.0), axis=0, keepdims=True)


def ffn_decode(x, g, w_in, w_out, gf, q, cache_kt, cache_vt, page_table, bias, layer, *, tf, final):
    t, d = x.shape
    dff = w_out.shape[1]
    bs, n_pages = page_table.shape
    page = cache_kt.shape[3]
    tm = t // bs
    assert page == SB_TK and tm * bs == t and tm % 8 == 0

    def page_spec(g):
        return pl.BlockSpec((None, None, SB_W, page), lambda i, pt: (layer, pt[i * n_pages + g], 0, 0))

    grid_spec = pltpu.PrefetchScalarGridSpec(
        num_scalar_prefetch=1,
        grid=(bs,),
        in_specs=_ffn_in_specs(tm, d, dff, layer)
        + [pl.BlockSpec((1, 1, SB_W), lambda i, pt: (i, 0, 0)), _resident((SB_HEADS, 1)),
           _resident((2 * SB_TK, 2 * SB_TK))]
        + [page_spec(g) for g in range(n_pages)] * 2,
        out_specs=[pl.BlockSpec((tm, d), lambda i, pt: (i, 0)), pl.BlockSpec((1, 1, SB_W), lambda i, pt: (i, 0, 0))],
        scratch_shapes=[pltpu.VMEM((tm, dff), BF16)],
    )
    y, o = pl.pallas_call(
        functools.partial(_ffn_decode_kernel, n_pages=n_pages, final=final, tf=tf),
        grid_spec=grid_spec,
        out_shape=[jax.ShapeDtypeStruct((t, d), F32), jax.ShapeDtypeStruct((bs, 1, SB_W), F32)],
        compiler_params=_cparams(("parallel",)),
        name="ffn_decode",
    )(page_table.reshape(-1), x, g.reshape(1, d), w_in, w_out, gf.reshape(1, d), q.reshape(bs, 1, SB_W),
      bias.reshape(SB_HEADS, 1), _suffix_matrix(), *([cache_kt] * n_pages), *([cache_vt] * n_pages))
    return y, o.reshape(bs, SB_W)


def _head_indicator(width, head):
    i = np.arange(width) // head
    return jnp.asarray(i[:, None] == i[None, :], BF16)


def _gdn_qkv_post(conv_out, seg):
    act = conv_out * _sigmoid(conv_out)
    q = act[:, :GDN_QK_W]
    k = act[:, GDN_QK_W:2 * GDN_QK_W]
    v = act[:, 2 * GDN_QK_W:]
    qn = q * lax.rsqrt(_dot_exact_rhs(q * q, seg) + L2_EPS) * (GDN_DK ** -0.5)
    kn = k * lax.rsqrt(_dot_exact_rhs(k * k, seg) + L2_EPS)
    return qn, kn, v


def _gdn_gates(ab, alog, dtb):
    g = -jnp.exp(alog) * _softplus(ab + dtb)
    return g, _sigmoid(ab)


def _gdn_pre_kernel(x_ref, halo_ref, ab_ref, cw_ref, alog_ref, dtb_ref, seg_ref, q_ref, k_ref, v_ref, gb_ref):
    i = pl.program_id(1)
    x = x_ref[0]
    tl = x.shape[0]
    halo = jnp.where(i > 0, halo_ref[0], 0.0)
    nh = halo.shape[0]
    ext = jnp.concatenate([halo, x], axis=0)
    out = None
    for tap in range(GDN_CONV):
        d = GDN_CONV - 1 - tap
        src = x if d == 0 else pltpu.roll(ext, d, axis=0)[nh:]
        term = src * cw_ref[tap:tap + 1, :]
        out = term if out is None else out + term
    qn, kn, v = _gdn_qkv_post(out, seg_ref[...])
    q_ref[0] = qn
    k_ref[0] = kn
    v_ref[0] = v

    g, beta = _gdn_gates(ab_ref[0], alog_ref[...], dtb_ref[...])
    tt = lax.broadcasted_iota(jnp.int32, g.shape, 0) % GDN_CHUNK
    k = 1
    while k < GDN_CHUNK:
        g = g + jnp.where(tt >= k, pltpu.roll(g, k, axis=0), 0.0)
        k *= 2
    lane = lax.broadcasted_iota(jnp.int32, g.shape, 1)
    gb_ref[0] = jnp.where(lane < GDN_HEADS, g, beta)


def gdn_pre(p3, conv_w, alog, dtb, *, tl):
    b, l, _ = p3.shape
    nh = 8
    out_sd = jax.ShapeDtypeStruct((b, l, GDN_QK_W), F32)
    return pl.pallas_call(
        _gdn_pre_kernel,
        grid=(b, l // tl),
        in_specs=[pl.BlockSpec((1, tl, GDN_QKV_W), lambda n, i: (n, i, P_GQKV // GDN_QKV_W)),
                  pl.BlockSpec((1, nh, GDN_QKV_W), lambda n, i: (n, jnp.maximum(i * (tl // nh) - 1, 0), 0)),
                  pl.BlockSpec((1, tl, LANES), lambda n, i: (n, i, P_AB // LANES)),
                  pl.BlockSpec((GDN_CONV, GDN_QKV_W), lambda n, i: (0, 0)),
                  pl.BlockSpec((1, LANES), lambda n, i: (0, 0)),
                  pl.BlockSpec((1, LANES), lambda n, i: (0, 0)),
                  pl.BlockSpec((GDN_QK_W, GDN_QK_W), lambda n, i: (0, 0))],
        out_specs=[pl.BlockSpec((1, tl, GDN_QK_W), lambda n, i: (n, i, 0))] * 3
        + [pl.BlockSpec((1, tl, LANES), lambda n, i: (n, i, 0))],
        out_shape=[out_sd, out_sd, out_sd, jax.ShapeDtypeStruct((b, l, LANES), F32)],
        compiler_params=_cparams(("parallel", "parallel")),
        name="gdn_pre",
    )(p3, p3, p3, conv_w, alog, dtb, _head_indicator(GDN_QK_W, GDN_DK))


def _gate_broadcast_matrix():
    r = np.arange(2 * LANES)[:, None] % LANES
    c = np.arange(2 * LANES)[None, :]
    return jnp.asarray(np.where(c < LANES, r < GDN_HEADS, (r >= GDN_HEADS) & (r < 2 * GDN_HEADS)), BF16)


def _gdn_chunk_terms(q, k, v, gb, e_mat):
    ch = GDN_CHUNK
    r = GDN_HEADS * ch
    sh = ch.bit_length() - 1
    chunks = range(len(q))
    ri = lax.broadcasted_iota(jnp.int32, (r, r), 0)
    ci = lax.broadcasted_iota(jnp.int32, (r, r), 1)
    same_head = (ri >> sh) == (ci >> sh)
    incl = same_head & (ri >= ci)
    strict = same_head & (ri > ci)
    eye = (ri == ci).astype(F32)

    def stack(x):
        return jnp.where(same_head, jnp.concatenate([x] * GDN_HEADS, axis=0), 0.0)

    def widen(x):
        return jnp.concatenate([x] * (r // LANES), axis=1)

    kbd, qbd, vbd = [stack(x) for x in k], [stack(x) for x in q], [stack(x) for x in v]
    yield

    l1 = lax.broadcasted_iota(jnp.int32, (r, LANES), 1)
    r1 = lax.broadcasted_iota(jnp.int32, (r, LANES), 0) >> sh
    gate_lane = (l1 == r1) | (l1 == r1 + GDN_HEADS)
    gsel = [_split(jnp.where(gate_lane, jnp.concatenate([x] * GDN_HEADS, axis=0), 0.0)) for x in gb]
    gbc = [_dot(jnp.concatenate([hi, lo], axis=1), e_mat) for hi, lo in gsel]
    g1 = [x[:, :LANES] for x in gbc]
    gl1 = [jnp.concatenate([jnp.broadcast_to(x[(h + 1) * ch - 1:(h + 1) * ch, :], (ch, LANES))
                            for h in range(GDN_HEADS)], axis=0) for x in g1]
    gcol = [widen(x) for x in g1]
    bcol = [widen(x[:, LANES:]) for x in gbc]
    eg = [widen(jnp.exp(x)) for x in g1]
    egl = [widen(jnp.exp(x)) for x in gl1]
    ekd = [widen(jnp.exp(gl1[i] - g1[i])) for i in chunks]

    yield
    decay = [jnp.where(incl, jnp.exp(jnp.where(incl, x - x.T, 0.0)), 0.0) for x in gcol]
    kb = [kbd[i] * bcol[i] for i in chunks]
    kbd_b = [x.astype(BF16) for x in kbd]
    a_mat = [jnp.where(strict, _dot_nt(kb[i].astype(BF16), kbd_b[i]) * decay[i], 0.0) for i in chunks]
    yield

    t_inv = [eye - jnp.where((ri >> 1) == (ci >> 1), x, 0.0) for x in a_mat]
    for lv in range(1, sh):
        off = ((ri >> (lv + 1)) == (ci >> (lv + 1))) & ((ri >> lv) != (ci >> lv))
        tb = [x.astype(BF16) for x in t_inv]
        m = [_dot(jnp.where(off, a_mat[i], 0.0).astype(BF16), tb[i]).astype(BF16) for i in chunks]
        yield
        t_inv = [t_inv[i] - _dot(tb[i], m[i]) for i in chunks]
        yield
    res = [(eye - t_inv[i]) - _mm3(a_mat[i], t_inv[i]) for i in chunks]
    yield
    t_inv = [t_inv[i] + _dot(t_inv[i].astype(BF16), res[i].astype(BF16)) for i in chunks]
    t_split = [_split(x) for x in t_inv]
    yield

    def apply_t(i, x):
        th, tl = t_split[i]
        xh, xl = _split(x)
        return _dot(th, xh) + (_dot(th, xl) + _dot(tl, xh))

    u = [apply_t(i, vbd[i] * bcol[i]) for i in chunks]
    yield
    w = [apply_t(i, kb[i] * eg[i]).astype(BF16) for i in chunks]
    yield
    qk = [jnp.where(incl, _dot_nt(qbd[i].astype(BF16), kbd_b[i]) * decay[i], 0.0).astype(BF16) for i in chunks]
    qd = [(qbd[i] * eg[i]).astype(BF16) for i in chunks]
    kd = [(kbd[i] * ekd[i]).astype(BF16) for i in chunks]
    return list(zip(u, w, qk, qd, kd, egl))


def _gdn_chunk_kernel(q_ref, k_ref, v_ref, gb_ref, z_ref, gn_ref, e_ref, seg_ref, og_ref, s_ref, st_ref, *, nc):
    c = pl.program_id(1)
    ch = GDN_CHUNK

    @pl.when(c == 0)
    def _():
        st_ref[...] = jnp.zeros_like(st_ref)

    def terms_of(chunk_ids):
        rows = [slice(i * ch, (i + 1) * ch) for i in chunk_ids]
        return _gdn_chunk_terms([q_ref[0, x, :] for x in rows], [k_ref[0, x, :] for x in rows],
                                [v_ref[0, x, :] for x in rows], [gb_ref[0, x, :] for x in rows], e_ref[...])

    state = {"s": st_ref[...]}
    outs = []

    def recurrence(terms):
        for u, w, qk, qd, kd, egl in terms:
            sb = state["s"].astype(BF16)
            vnb = (u - _dot(w, sb)).astype(BF16)
            o = _dot(qd, sb)
            yield
            o = o + _dot(qk, vnb)
            state["s"] = state["s"] * egl + _dot_tn(kd, vnb)
            oc = o[0:ch]
            for h in range(1, GDN_HEADS):
                oc = oc + o[h * ch:(h + 1) * ch]
            outs.append(oc)
            yield

    half = max(nc // 2, 1)
    terms_a, = _alternate(terms_of(range(half)))
    if nc > half:
        terms_b, _ = _alternate(terms_of(range(half, nc)), recurrence(terms_a))
        _alternate(recurrence(terms_b))
    else:
        _alternate(recurrence(terms_a))
    s = state["s"]
    st_ref[...] = s

    oc = jnp.concatenate(outs, axis=0)
    ms = _dot_exact_rhs(oc * oc, seg_ref[...]) * (1.0 / GDN_DV)
    zz = z_ref[0]
    og_ref[0] = oc * lax.rsqrt(ms + RMS_EPS) * gn_ref[...] * (zz * _sigmoid(zz))

    @pl.when(c == pl.num_programs(1) - 1)
    def _():
        for h in range(GDN_HEADS):
            s_ref[0, h] = s[h * GDN_DK:(h + 1) * GDN_DK, h * GDN_DV:(h + 1) * GDN_DV]


def gdn_chunk(qn, kn, v, gb, p3, gnorm, *, nc):
    b, l, _ = qn.shape
    ch = GDN_CHUNK * nc
    r = GDN_HEADS * GDN_CHUNK
    blk = lambda w: pl.BlockSpec((1, ch, w), lambda n, c: (n, c, 0))
    return pl.pallas_call(
        functools.partial(_gdn_chunk_kernel, nc=nc),
        grid=(b, l // ch),
        in_specs=[blk(GDN_QK_W), blk(GDN_QK_W), blk(GDN_V_W), blk(LANES),
                  pl.BlockSpec((1, ch, GDN_V_W), lambda n, c: (n, c, P_GZ // GDN_V_W)),
                  pl.BlockSpec((1, GDN_V_W), lambda n, c: (0, 0)),
                  pl.BlockSpec((2 * LANES, 2 * LANES), lambda n, c: (0, 0)),
                  pl.BlockSpec((GDN_V_W, GDN_V_W), lambda n, c: (0, 0))],
        out_specs=[pl.BlockSpec((1, ch, GDN_V_W), lambda n, c: (n, c, 0)),
                   pl.BlockSpec((1, GDN_HEADS, GDN_DK, GDN_DV), lambda n, c: (n, 0, 0, 0))],
        out_shape=[jax.ShapeDtypeStruct((b, l, GDN_V_W), F32),
                   jax.ShapeDtypeStruct((b, GDN_HEADS, GDN_DK, GDN_DV), F32)],
        scratch_shapes=[pltpu.VMEM((r, r), F32)],
        compiler_params=_cparams(("parallel", "arbitrary")),
        name="gdn_chunk",
    )(qn, kn, v, gb, p3, jnp.tile(gnorm, GDN_HEADS).reshape(1, GDN_V_W), _gate_broadcast_matrix(),
      _head_indicator(GDN_V_W, GDN_DV))


def _sample_pre_kernel(up_ref, x_ref, ab_ref, z_ref, ph_ref, ch_ref, pw_ref, psc_ref, cw_ref, alog_ref, dtb_ref,
                       seg_ref, *refs, pos0, carried):
    op_ref, qt_ref, kt_ref, vt_ref, zt_ref, gbt_ref, pn_ref, cn_ref = refs[2 * carried:]
    u = up_ref[...]
    hist = ph_ref.shape[0]
    grp = lax.broadcasted_iota(jnp.int32, u.shape, 1) // POOL_GC
    tails, tail = {0: jnp.zeros_like(u)}, jnp.zeros_like(u)
    for n in range(1, hist + 1):
        tail = tail + ph_ref[hist - n]
        tails[n] = tail
    means = [(tails[min(win - 1, hist)] + u) / float(min(win, pos0 + 1)) for win in POOL_WINDOWS]
    pooled = _pool_select(grp, means) - u
    op_ref[...] = _dot(pooled.astype(BF16), pw_ref[...]) * psc_ref[...]
    pn_ref[0:hist - 1] = ph_ref[1:hist]
    pn_ref[hist - 1] = u

    x = x_ref[...]
    out = ch_ref[0] * cw_ref[0:1, :]
    for tap in range(1, GDN_CONV - 1):
        out = out + ch_ref[tap] * cw_ref[tap:tap + 1, :]
    out = out + x * cw_ref[GDN_CONV - 1:, :]
    cn_ref[0:GDN_CONV - 2] = ch_ref[1:GDN_CONV - 1]
    cn_ref[GDN_CONV - 2] = x
    qn, kn, v = _gdn_qkv_post(out, seg_ref[...])
    qt_ref[...] = qn.T
    kt_ref[...] = kn.T
    vt_ref[...] = v.T
    zt_ref[...] = z_ref[...].T

    g, beta = _gdn_gates(ab_ref[...], alog_ref[...], dtb_ref[...])
    lane = lax.broadcasted_iota(jnp.int32, g.shape, 1)
    gbt_ref[...] = jnp.where(lane < GDN_HEADS, jnp.exp(g), beta).T


def sample_pre(p, pool_hist, conv_hist, pool_new, conv_new, layer, pool_w_bd, pool_scale, conv_w, alog, dtb, *, pos0):
    bs = p.shape[0]
    depth = pool_hist.shape[0]
    full = lambda shape: pl.BlockSpec(shape, lambda i: (0,) * len(shape))
    of_layer = lambda shape: pl.BlockSpec((None, *shape), lambda i: (layer,) + (0,) * len(shape))
    sd = lambda *shape: jax.ShapeDtypeStruct(shape, F32)
    pool_rows, conv_rows = pool_hist.shape[1:], conv_hist.shape[1:]
    carried = pool_new is not None
    outs = pl.pallas_call(
        functools.partial(_sample_pre_kernel, pos0=pos0, carried=carried),
        grid=(1,),
        in_specs=[pl.BlockSpec((bs, POOL_W), lambda i: (0, P_POOL // POOL_W)),
                  pl.BlockSpec((bs, GDN_QKV_W), lambda i: (0, P_GQKV // GDN_QKV_W)),
                  pl.BlockSpec((bs, LANES), lambda i: (0, P_AB // LANES)),
                  pl.BlockSpec((bs, GDN_V_W), lambda i: (0, P_GZ // GDN_V_W)),
                  of_layer(pool_rows), of_layer(conv_rows), of_layer((POOL_W, POOL_W)), full((1, POOL_W)),
                  full((GDN_CONV, GDN_QKV_W)), full((1, LANES)), full((1, LANES)), full((GDN_QK_W, GDN_QK_W))]
        + [pl.BlockSpec(memory_space=pl.ANY)] * (2 * carried),
        out_specs=[full((bs, POOL_W)), full((GDN_QK_W, bs)), full((GDN_QK_W, bs)), full((GDN_V_W, bs)),
                   full((GDN_V_W, bs)), full((LANES, bs)), of_layer(pool_rows), of_layer(conv_rows)],
        out_shape=[sd(bs, POOL_W), sd(GDN_QK_W, bs), sd(GDN_QK_W, bs), sd(GDN_V_W, bs), sd(GDN_V_W, bs),
                   sd(LANES, bs), sd(depth, *pool_rows), sd(depth, *conv_rows)],
        input_output_aliases={12: 6, 13: 7} if carried else {},
        compiler_params=_cparams(("arbitrary",)),
        name="sample_pre",
    )(p, p, p, p, pool_hist, conv_hist, pool_w_bd, pool_scale.reshape(1, POOL_W), conv_w, alog, dtb,
      _head_indicator(GDN_QK_W, GDN_DK), *([pool_new, conv_new] if carried else []))
    return outs


def _gdn_step_kernel(s_ref, k_ref, q_ref, v_ref, z_ref, gb_ref, gn_ref, *refs):
    so_ref, og_ref = refs[-2:]
    h = pl.program_id(0)
    eg = gb_ref[pl.ds(h, 1), :]
    beta = gb_ref[pl.ds(h + GDN_HEADS, 1), :]
    ks = None
    for dk in range(GDN_DK):
        term = k_ref[dk:dk + 1, :] * (s_ref[dk] * eg)
        ks = term if ks is None else ks + term
    delta = (v_ref[...] - ks) * beta
    o = None
    for dk in range(GDN_DK):
        s_new = s_ref[dk] * eg + k_ref[dk:dk + 1, :] * delta
        so_ref[dk] = s_new
        term = q_ref[dk:dk + 1, :] * s_new
        o = term if o is None else o + term
    ms = jnp.mean(o * o, axis=0, keepdims=True)
    zz = z_ref[...]
    og_ref[...] = o * lax.rsqrt(ms + RMS_EPS) * gn_ref[...] * (zz * _sigmoid(zz))


def gdn_step(state, state_new, layer, kt, qt, vt, zt, gbt, gnorm):
    depth, h, dk, dv, bs = state.shape
    of_head = lambda rows: pl.BlockSpec((rows, bs), lambda i: (i, 0))
    s_spec = pl.BlockSpec((None, None, dk, dv, bs), lambda i: (layer, i, 0, 0, 0))
    carried = state_new is not None
    return pl.pallas_call(
        _gdn_step_kernel,
        grid=(h,),
        in_specs=[s_spec, of_head(dk), of_head(dk), of_head(dv), of_head(dv),
                  pl.BlockSpec((LANES, bs), lambda i: (0, 0)), pl.BlockSpec((dv, 1), lambda i: (0, 0))]
        + [pl.BlockSpec(memory_space=pl.ANY)] * carried,
        out_specs=[s_spec, of_head(dv)],
        out_shape=[jax.ShapeDtypeStruct(state.shape, F32), jax.ShapeDtypeStruct((h * dv, bs), F32)],
        input_output_aliases={7: 0} if carried else {},
        compiler_params=_cparams(("parallel",)),
        name="gdn_step",
    )(state, kt, qt, vt, zt, gbt, gnorm.reshape(dv, 1), *([state_new] if carried else []))


def _largest_tile(n, cap):
    t = min(n, cap)
    while n % t:
        t //= 2
    return t


def _pack_w_in(w_in):
    depth, d, _ = w_in.shape
    pad = jnp.zeros((depth, d, P_W - P_AB - (R_GZ - R_GA)), w_in.dtype)
    return jnp.concatenate([w_in[..., R_GQKV:R_GA], w_in[..., R_POOL:R_GQKV], w_in[..., R_GZ:R_GATE],
                            w_in[..., R_GA:R_GZ], pad], axis=-1).astype(BF16)


def _lane_pad(x):
    return jnp.pad(x, ((0, 0), (0, LANES - x.shape[1])))


def kernel(x_prompt, x_sample, cache_sb_k, cache_sb_v, page_table, state_pool, state_gdn_conv, state_gdn, norm1_g, w_in, pool_w, pool_scale, sb_bias, gdn_conv_w, gdn_a_log, gdn_dt_bias, gdn_norm_g, w_branch, w_o, norm2_g, w_ffn_in, w_ffn_out, normf_g):
    bp, seq, d = x_prompt.shape
    bs = x_sample.shape[0]
    depth = w_in.shape[0]
    n_phys, page = cache_sb_k.shape[1], cache_sb_k.shape[2]
    past_len = page_table.shape[1] * page
    tp = bp * seq

    w_in_p = _pack_w_in(w_in)
    w_gate_b = w_in[..., R_GATE:R_END].astype(BF16)
    w_branch_b = w_branch.astype(BF16)
    w_o_b = w_o.astype(BF16)
    w_ffn_in_b = w_ffn_in.astype(BF16)
    w_ffn_out_b = w_ffn_out.astype(BF16)
    eye_g = jnp.eye(len(POOL_WINDOWS), dtype=F32)
    pool_w_bd = jnp.einsum("lgcd,gh->lgchd", pool_w, eye_g).reshape(depth, POOL_W, POOL_W).astype(BF16)
    alog_p = _lane_pad(gdn_a_log)
    dtb_p = _lane_pad(gdn_dt_bias)
    cache_kt = jnp.transpose(cache_sb_k, (0, 1, 3, 4, 2)).reshape(depth, n_phys, SB_W, page)
    cache_vt = jnp.transpose(cache_sb_v, (0, 1, 3, 4, 2)).reshape(depth, n_phys, SB_W, page)
    n_pages = page_table.shape[1]
    page_group = _largest_tile(n_pages, 16)
    fuse_decode = page_group == n_pages and tp % bs == 0 and (tp // bs) % 8 == 0

    tm_p = _largest_tile(tp, 512)
    tm_merge = _largest_tile(tp, 512)
    tm_s = _largest_tile(bs, 128)
    tq = _largest_tile(seq, 256)
    tl = _largest_tile(seq, 256)

    xp = x_prompt.reshape(tp, d)
    xs = x_sample.reshape(bs, d)
    pool_hist = jnp.transpose(state_pool, (0, 2, 1, 3))
    conv_hist = jnp.transpose(state_gdn_conv, (0, 2, 1, 3))
    gdn_state = jnp.transpose(state_gdn, (0, 2, 3, 4, 1))
    pool_s = conv_s = gdn_s = None

    pool_p, conv_p, gdn_p, k_s, v_s = [], [], [], [], []
    kt_new = vt_new = None
    for l in range(depth):
        final = l == depth - 1
        p, kt_new, vt_new = rms_matmul(xp, norm1_g[l], w_in_p, l, tm=tm_p, tn=1024,
                                       kv_t=(depth, seq, kt_new, vt_new))
        p3 = p.reshape(bp, seq, P_W)
        o_pool = pool_prompt(p3, pool_w_bd[l], pool_scale[l])
        o_sb = sb_prompt(p3, sb_bias[l], tq=tq)
        qn, kn, gv, gb = gdn_pre(p3, gdn_conv_w[l], alog_p[l:l + 1], dtb_p[l:l + 1], tl=tl)
        o_g, s_p = gdn_chunk(qn, kn, gv, gb, p3, gdn_norm_g[l], nc=_largest_tile(seq // GDN_CHUNK, 8))
        xp = merge(xp, norm1_g[l], o_pool.reshape(tp, POOL_W), o_sb.reshape(tp, SB_W), o_g.reshape(tp, GDN_V_W),
                   w_gate_b, w_branch_b, w_o_b, l, tm=tm_merge)
        ps = rms_matmul(xs, norm1_g[l], w_in_p, l, tm=tm_s, tn=1024)
        if fuse_decode:
            xp, o_sb_s = ffn_decode(xp, norm2_g[l], w_ffn_in_b, w_ffn_out_b, normf_g, ps[:, P_SBQ:P_SBK],
                                    cache_kt, cache_vt, page_table, sb_bias[l], l, tf=256, final=final)
        else:
            xp = ffn(xp, norm2_g[l], w_ffn_in_b, w_ffn_out_b, normf_g, l, tm=tm_p, tf=256, final=final)
            o_sb_s = sb_decode(ps[:, P_SBQ:P_SBK], cache_kt, cache_vt, page_table, sb_bias[l], l, group=page_group)
        pool_p.append(p3[:, seq - POOL_HIST:, P_POOL:P_SBQ])
        conv_p.append(p3[:, seq - (GDN_CONV - 1):, P_GQKV:P_POOL])
        gdn_p.append(s_p)

        o_pool_s, qt_s, kt_s, vt_s, zt_s, gbt_s, pool_s, conv_s = sample_pre(
            ps, pool_hist, conv_hist, pool_s, conv_s, l, pool_w_bd, pool_scale[l], gdn_conv_w[l], alog_p[l:l + 1],
            dtb_p[l:l + 1], pos0=past_len)
        gdn_s, ogt_s = gdn_step(gdn_state, gdn_s, l, kt_s, qt_s, vt_s, zt_s, gbt_s, gdn_norm_g[l])
        xs = merge(xs, norm1_g[l], o_pool_s, o_sb_s, ogt_s.T, w_gate_b, w_branch_b, w_o_b, l, tm=tm_s)
        xs = ffn(xs, norm2_g[l], w_ffn_in_b, w_ffn_out_b, normf_g, l, tm=tm_s, tf=256, final=final)
        k_s.append(ps[:, P_SBK:P_SBV].reshape(bs, 1, SB_HEADS, SB_HEAD_DIM))
        v_s.append(ps[:, P_SBV:P_GZ].reshape(bs, 1, SB_HEADS, SB_HEAD_DIM))

    y_prompt = xp.reshape(bp, seq, d)
    y_sample = xs.reshape(bs, 1, d)
    new_kv = [jnp.transpose(t.reshape(depth, bp, SB_HEADS, SB_HEAD_DIM, seq), (0, 1, 4, 2, 3))
              for t in (kt_new, vt_new)]
    return (y_prompt, y_sample, *new_kv, jnp.stack(pool_p), jnp.stack(conv_p), jnp.stack(gdn_p),
            jnp.stack(k_s), jnp.stack(v_s), jnp.transpose(pool_s, (0, 2, 1, 3)), jnp.transpose(conv_s, (0, 2, 1, 3)),
            jnp.transpose(gdn_s, (0, 4, 1, 2, 3)))
```

```python
import functools

import numpy as np
import jax
import jax.numpy as jnp
from jax import lax
from jax.experimental import pallas as pl
from jax.experimental.pallas import tpu as pltpu

F32 = jnp.float32
BF16 = jnp.bfloat16

D_MODEL = 1024
POOL_WINDOWS = (2, 4, 8, 16)
POOL_W = 256
POOL_GC = 64
POOL_HIST = 15
SB_HEADS = 8
SB_HEAD_DIM = 64
SB_W = 512
GDN_HEADS = 4
GDN_DK = 64
GDN_DV = 64
GDN_QK_W = 256
GDN_V_W = 256
GDN_QKV_W = 768
GDN_CONV = 4
GDN_CHUNK = 64
D_FF = 2816
RMS_EPS = 1e-6
L2_EPS = 1e-6

R_POOL, R_SBQ, R_GQKV, R_GA, R_GZ, R_GATE, R_END = 0, 256, 1792, 2560, 2568, 2824, 5896
P_GQKV, P_POOL, P_SBQ, P_SBK, P_SBV, P_GZ, P_AB, P_W = 0, 768, 1024, 1536, 2048, 2560, 2816, 3072

LANES = 128
SB_TK = LANES
VMEM_LIMIT = 56 * 1024 * 1024


def _cparams(sem):
    return pltpu.CompilerParams(dimension_semantics=sem, vmem_limit_bytes=VMEM_LIMIT)


def _sigmoid(x):
    return 1.0 / (1.0 + jnp.exp(-x))


NEG_LOG2E = -1.4426950408889634


def _softplus(x):
    return jnp.maximum(x, 0.0) + jnp.log(1.0 + jnp.exp2(jnp.abs(x) * NEG_LOG2E))


def _split(x):
    hi = x.astype(BF16)
    lo = (x - hi.astype(F32)).astype(BF16)
    return hi, lo


def _dot(a, b):
    return jnp.dot(a, b, preferred_element_type=F32)


def _dot_nt(a, b):
    return lax.dot_general(a, b, (((1,), (1,)), ((), ())), preferred_element_type=F32)


def _dot_tn(a, b):
    return lax.dot_general(a, b, (((0,), (0,)), ((), ())), preferred_element_type=F32)


def _mm3(a, b):
    ah, al = _split(a)
    bh, bl = _split(b)
    return _dot(ah, bh) + (_dot(ah, bl) + _dot(al, bh))


def _dot_exact_rhs(x, m):
    hi, lo = _split(x)
    return _dot(hi, m) + _dot(lo, m)


def _rms_scale(x):
    return lax.rsqrt(jnp.mean(x * x, axis=-1, keepdims=True) + RMS_EPS)


def _alternate(*stage_gens):
    results = [None] * len(stage_gens)
    live = list(range(len(stage_gens)))
    while live:
        for i in list(live):
            try:
                next(stage_gens[i])
            except StopIteration as done:
                results[i] = done.value
                live.remove(i)
    return results


def _resident(shape, layer=None):
    if layer is None:
        return pl.BlockSpec(shape, lambda *_: (0,) * len(shape), pipeline_mode=pl.Buffered(1))
    return pl.BlockSpec((None, *shape), lambda *_: (layer,) + (0,) * len(shape), pipeline_mode=pl.Buffered(1))


def _rms_matmul_kernel(x_ref, g_ref, w_ref, *refs, tn, kv_t):
    o_ref = refs[-3] if kv_t else refs[-1]
    x = x_ref[...]
    h = (x * _rms_scale(x) * g_ref[...]).astype(BF16)
    for j in range(w_ref.shape[1] // tn):
        o_ref[:, j * tn:(j + 1) * tn] = _dot(h, w_ref[:, j * tn:(j + 1) * tn])
    if kv_t:
        kt_ref, vt_ref = refs[-2:]
        kt_ref[0] = o_ref[:, P_SBK:P_SBV].T
        vt_ref[0] = o_ref[:, P_SBV:P_GZ].T


def rms_matmul(x, g, w, layer, *, tm, tn, kv_t=None):
    t, d = x.shape
    n = w.shape[2]
    in_specs = [pl.BlockSpec((tm, d), lambda i: (i, 0)), _resident((1, d)), _resident((d, n), layer)]
    out_specs = [pl.BlockSpec((tm, n), lambda i: (i, 0))]
    out_shape = [jax.ShapeDtypeStruct((t, n), F32)]
    operands = [x, g.reshape(1, d), w]
    aliases = {}
    if kv_t:
        depth, seq, k_buf, v_buf = kv_t
        tiles = seq // tm
        t_spec = pl.BlockSpec((None, 1, SB_W, tm), lambda i: (layer, i // tiles, 0, i % tiles))
        out_specs += [t_spec, t_spec]
        out_shape += [jax.ShapeDtypeStruct((depth, t // seq, SB_W, seq), F32)] * 2
        if k_buf is not None:
            in_specs += [pl.BlockSpec(memory_space=pl.ANY)] * 2
            operands += [k_buf, v_buf]
            aliases = {3: 1, 4: 2}
    out = pl.pallas_call(
        functools.partial(_rms_matmul_kernel, tn=tn, kv_t=bool(kv_t)),
        grid=(t // tm,),
        in_specs=in_specs,
        out_specs=out_specs,
        out_shape=out_shape,
        input_output_aliases=aliases,
        compiler_params=_cparams(("parallel",)),
        name="in_proj",
    )(*operands)
    return out if kv_t else out[0]


def _ffn_stages(x_ref, g_ref, wi_ref, wo_ref, gf_ref, o_ref, act_ref, *, final, tf):
    x = x_ref[...]
    h = (x * _rms_scale(x) * g_ref[...]).astype(BF16)
    dff = wo_ref.shape[0]
    for f in range(dff // tf):
        gate = _dot(h, wi_ref[:, f * tf:(f + 1) * tf])
        up = _dot(h, wi_ref[:, dff + f * tf:dff + (f + 1) * tf])
        act_ref[:, f * tf:(f + 1) * tf] = ((gate * _sigmoid(gate)) * up).astype(BF16)
        yield
    y = x + _dot(act_ref[...], wo_ref[...])
    if final:
        y = y * _rms_scale(y) * gf_ref[...]
    o_ref[...] = y


def _ffn_kernel(*refs, final, tf):
    _alternate(_ffn_stages(*refs, final=final, tf=tf))


def _ffn_in_specs(tm, d, dff, layer):
    return [pl.BlockSpec((tm, d), lambda i, *_: (i, 0)), _resident((1, d)), _resident((d, 2 * dff), layer),
            _resident((dff, d), layer), _resident((1, d))]


def ffn(x, g, w_in, w_out, gf, layer, *, tm, tf, final):
    t, d = x.shape
    dff = w_out.shape[1]
    return pl.pallas_call(
        functools.partial(_ffn_kernel, final=final, tf=tf),
        grid=(t // tm,),
        in_specs=_ffn_in_specs(tm, d, dff, layer),
        out_specs=pl.BlockSpec((tm, d), lambda i: (i, 0)),
        out_shape=jax.ShapeDtypeStruct((t, d), F32),
        scratch_shapes=[pltpu.VMEM((tm, dff), BF16)],
        compiler_params=_cparams(("parallel",)),
        name="ffn",
    )(x, g.reshape(1, d), w_in, w_out, gf.reshape(1, d))


def _merge_kernel(x_ref, g_ref, op_ref, osb_ref, og_ref, wg_ref, wb_ref, wo_ref, o_ref):
    x = x_ref[...]
    d = x.shape[1]
    h = (x * _rms_scale(x) * g_ref[...]).astype(BF16)
    m = None
    row = 0
    for i, br_ref in enumerate((op_ref, osb_ref, og_ref)):
        width = br_ref.shape[1]
        branch = _dot(br_ref[...].astype(BF16), wb_ref[row:row + width, :])
        term = _sigmoid(_dot(h, wg_ref[:, i * d:(i + 1) * d])) * branch
        m = term if m is None else m + term
        row += width
    o_ref[...] = x + _dot(m.astype(BF16), wo_ref[...])


def merge(x, g, o_pool, o_sb, o_g, w_gate, w_branch, w_o, layer, *, tm):
    t, d = x.shape
    return pl.pallas_call(
        _merge_kernel,
        grid=(t // tm,),
        in_specs=[pl.BlockSpec((tm, d), lambda i: (i, 0)), _resident((1, d)),
                  pl.BlockSpec((tm, POOL_W), lambda i: (i, 0)),
                  pl.BlockSpec((tm, SB_W), lambda i: (i, 0)),
                  pl.BlockSpec((tm, GDN_V_W), lambda i: (i, 0)),
                  _resident((d, 3 * d), layer), _resident((d, d), layer), _resident((d, d), layer)],
        out_specs=pl.BlockSpec((tm, d), lambda i: (i, 0)),
        out_shape=jax.ShapeDtypeStruct((t, d), F32),
        compiler_params=_cparams(("parallel",)),
        name="merge",
    )(x, g.reshape(1, d), o_pool, o_sb, o_g, w_gate, w_branch, w_o)


def _pool_select(lane_grp, vals):
    out = vals[-1]
    for gi in range(len(vals) - 2, -1, -1):
        out = jnp.where(lane_grp == gi, vals[gi], out)
    return out


def _pool_prompt_kernel(u_ref, w_ref, sc_ref, o_ref):
    u = u_ref[0]
    t = lax.broadcasted_iota(jnp.int32, u.shape, 0)
    grp = lax.broadcasted_iota(jnp.int32, u.shape, 1) // POOL_GC

    def shifted(x, k):
        return jnp.where(t >= k, pltpu.roll(x, k, axis=0), 0.0)

    sums = []
    s, w = u, 1
    for win in POOL_WINDOWS:
        while w < win:
            s = s + shifted(s, w)
            w *= 2
        sums.append(s)
    sel = _pool_select(grp, sums)
    win = _pool_select(grp, [jnp.full(u.shape, wn, jnp.int32) for wn in POOL_WINDOWS])
    cnt = jnp.minimum(win, t + 1).astype(F32)
    pooled = sel / cnt - u
    o_ref[0] = _dot(pooled.astype(BF16), w_ref[...]) * sc_ref[...]


def pool_prompt(p3, w_bd, scale):
    b, l, _ = p3.shape
    return pl.pallas_call(
        _pool_prompt_kernel,
        grid=(b,),
        in_specs=[pl.BlockSpec((1, l, POOL_W), lambda i: (i, 0, P_POOL // POOL_W)),
                  pl.BlockSpec((POOL_W, POOL_W), lambda i: (0, 0)),
                  pl.BlockSpec((1, POOL_W), lambda i: (0, 0))],
        out_specs=pl.BlockSpec((1, l, POOL_W), lambda i: (i, 0, 0)),
        out_shape=jax.ShapeDtypeStruct((b, l, POOL_W), F32),
        compiler_params=_cparams(("parallel",)),
        name="pool_prompt",
    )(p3, w_bd, scale.reshape(1, POOL_W))


def _sb_log_terms(z, vis, suffix_mat):
    sp = _softplus(z)
    hi, lo = _split(sp if vis is None else jnp.where(vis, sp, 0.0))
    return sp, _dot(jnp.concatenate([hi, lo], axis=1), suffix_mat)


def _head_pair_rows(x, first_head_lanes):
    return jnp.concatenate([jnp.where(first_head_lanes, x, 0.0), jnp.where(first_head_lanes, 0.0, x)],
                           axis=0).astype(BF16)


def _sb_prompt_kernel(bias_ref, q_ref, k_ref, v_ref, u2_ref, o_ref, qb_ref, acc_ref, c_ref, *, tq):
    qi = pl.program_id(1)
    tk = SB_TK
    hw = 2 * SB_HEAD_DIM
    npair = SB_HEADS // 2
    n_diag = tq // tk
    head0 = lax.broadcasted_iota(jnp.int32, (tk, hw), 1) < SB_HEAD_DIM
    row = lax.broadcasted_iota(jnp.int32, (tq, tk), 0)
    col = lax.broadcasted_iota(jnp.int32, (tq, tk), 1)
    qb_ref[...] = (q_ref[0] * (SB_HEAD_DIM ** -0.5)).astype(BF16)
    acc_ref[...] = jnp.zeros_like(acc_ref)
    c_ref[...] = jnp.zeros_like(c_ref)
    suffix_mat = u2_ref[...]

    pairs = [slice(p * hw, (p + 1) * hw) for p in range(npair)]

    def key_blocks(blocks):
        z2 = [[_dot_nt(qb_ref[r0:, lanes], _head_pair_rows(k_ref[0, pl.ds(ks, tk), lanes], head0)) for lanes in pairs]
              for ks, _, r0 in blocks]
        z = [[zb[h // 2][:, (h % 2) * tk:(h % 2 + 1) * tk] + bias_ref[h] for h in range(SB_HEADS)] for zb in z2]
        terms = [[_sb_log_terms(zh, vis, suffix_mat) for zh in zb] for zb, (_, vis, _) in zip(z, blocks)]
        a = [[None] * SB_HEADS for _ in blocks]
        for h in range(SB_HEADS):
            c = c_ref[h]
            for i, (_, vis, r0) in enumerate(blocks):
                sp, s2 = terms[i][h]
                ah = jnp.exp(z[i][h] - sp - s2[:, :tk] - c[r0:])
                a[i][h] = (ah if vis is None else jnp.where(vis, ah, 0.0)).astype(BF16)
                grown = c[r0:] + s2[:, tk:]
                c = grown if r0 == 0 else jnp.concatenate([c[:r0], grown], axis=0)
            c_ref[h] = c
        for p, lanes in enumerate(pairs):
            upd = {}
            for i, (ks, _, r0) in enumerate(blocks):
                vv = _head_pair_rows(v_ref[0, pl.ds(ks, tk), lanes], head0)
                d = _dot(jnp.concatenate(a[i][2 * p:2 * p + 2], axis=1), vv)
                upd[r0] = d if r0 not in upd else upd[r0] + d
            for r0, d in upd.items():
                acc_ref[r0:, lanes] += d

    key_blocks([(pl.multiple_of(qi * tq + d * tk, tk), ((col + d * tk) < row)[d * tk:], d * tk)
                for d in range(n_diag - 1, -1, -1)])

    def full_blocks(last, n):
        key_blocks([(pl.multiple_of((last - d) * tk, tk), None, 0) for d in range(n)])

    def full_body(jj, carry):
        full_blocks(qi * n_diag - 1 - jj * 2 * n_diag, 2 * n_diag)
        return carry

    lax.fori_loop(0, qi // 2, full_body, 0)

    @pl.when(qi % 2 == 1)
    def _():
        full_blocks(n_diag - 1, n_diag)

    o_ref[0] = acc_ref[...]


def _suffix_matrix():
    j = np.arange(2 * SB_TK)[:, None] % SB_TK
    c = np.arange(2 * SB_TK)[None, :]
    return jnp.asarray(np.where(c < SB_TK, j > c, True), BF16)


def sb_prompt(p3, bias, *, tq):
    b, l, _ = p3.shape
    return pl.pallas_call(
        functools.partial(_sb_prompt_kernel, tq=tq),
        grid=(b, l // tq),
        in_specs=[pl.BlockSpec(memory_space=pltpu.SMEM),
                  pl.BlockSpec((1, tq, SB_W), lambda i, q: (i, q, P_SBQ // SB_W)),
                  pl.BlockSpec((1, l, SB_W), lambda i, q: (i, 0, P_SBK // SB_W)),
                  pl.BlockSpec((1, l, SB_W), lambda i, q: (i, 0, P_SBV // SB_W)),
                  pl.BlockSpec((2 * SB_TK, 2 * SB_TK), lambda i, q: (0, 0))],
        out_specs=pl.BlockSpec((1, tq, SB_W), lambda i, q: (i, q, 0)),
        out_shape=jax.ShapeDtypeStruct((b, l, SB_W), F32),
        scratch_shapes=[pltpu.VMEM((tq, SB_W), BF16), pltpu.VMEM((tq, SB_W), F32),
                        pltpu.VMEM((SB_HEADS, tq, SB_TK), F32)],
        compiler_params=_cparams(("parallel", "arbitrary")),
        name="sb_prompt",
    )(bias, p3, p3, p3, _suffix_matrix())


def _sb_own_head():
    return (lax.broadcasted_iota(jnp.int32, (SB_HEADS, SB_W), 0)
            == lax.broadcasted_iota(jnp.int32, (SB_HEADS, SB_W), 1) // SB_HEAD_DIM)


def _sb_decode_stages(q_ref, bias_ref, u2_ref, k_refs, v_refs, carry, acc, *, per_stage=4):
    nh = SB_HEADS
    group = len(k_refs)
    q = jnp.broadcast_to(q_ref[0] * (SB_HEAD_DIM ** -0.5), (nh, SB_W))
    qbd = jnp.where(_sb_own_head(), q, 0.0).astype(BF16)
    zs = []
    for g in range(group):
        zs.append(_dot(qbd, k_refs[g][...].astype(BF16)) + bias_ref[...])
        if g % per_stage == per_stage - 1:
            yield
    z = jnp.concatenate(zs, axis=0)
    sp, s2 = _sb_log_terms(z, None, u2_ref[...])
    yield
    later = [None] * group
    for g in range(group - 1, -1, -1):
        later[g] = carry
        carry = carry + s2[g * nh:(g + 1) * nh, SB_TK:]
    a = jnp.exp(z - sp - s2[:, :SB_TK] - jnp.concatenate(later, axis=0))
    yield
    for g in range(group):
        acc = acc + _dot_nt(a[g * nh:(g + 1) * nh].astype(BF16), v_refs[g][...].astype(BF16))
        if g % per_stage == per_stage - 1:
            yield
    return carry, acc


def _sb_decode_kernel(pt_ref, q_ref, bias_ref, u2_ref, *refs, group):
    k_refs, v_refs = refs[:group], refs[group:2 * group]
    o_ref, c_ref, acc_ref = refs[2 * group:]
    step = pl.program_id(1)

    @pl.when(step == 0)
    def _():
        c_ref[...] = jnp.zeros_like(c_ref)
        acc_ref[...] = jnp.zeros_like(acc_ref)

    (carry, acc), = _alternate(_sb_decode_stages(q_ref, bias_ref, u2_ref, k_refs, v_refs, c_ref[...], acc_ref[...]))
    c_ref[...] = carry
    acc_ref[...] = acc

    @pl.when(step == pl.num_programs(1) - 1)
    def _():
        o_ref[0] = jnp.sum(jnp.where(_sb_own_head(), acc, 0.0), axis=0, keepdims=True)


def sb_decode(q, cache_kt, cache_vt, page_table, bias, layer, *, group):
    bs, n_pages = page_table.shape
    page = cache_kt.shape[3]
    assert page == SB_TK and n_pages % group == 0
    n_steps = n_pages // group

    def page_spec(g):
        def index(b, s, pt):
            return (layer, pt[b * n_pages + (n_steps - 1 - s) * group + g], 0, 0)
        return pl.BlockSpec((None, None, SB_W, page), index)

    grid_spec = pltpu.PrefetchScalarGridSpec(
        num_scalar_prefetch=1,
        grid=(bs, n_steps),
        in_specs=[pl.BlockSpec((1, 1, SB_W), lambda b, s, pt: (b, 0, 0)),
                  pl.BlockSpec((SB_HEADS, 1), lambda b, s, pt: (0, 0)),
                  pl.BlockSpec((2 * SB_TK, 2 * SB_TK), lambda b, s, pt: (0, 0))]
        + [page_spec(g) for g in range(group)] * 2,
        out_specs=pl.BlockSpec((1, 1, SB_W), lambda b, s, pt: (b, 0, 0)),
        scratch_shapes=[pltpu.VMEM((SB_HEADS, SB_TK), F32), pltpu.VMEM((SB_HEADS, SB_W), F32)],
    )
    out = pl.pallas_call(
        functools.partial(_sb_decode_kernel, group=group),
        grid_spec=grid_spec,
        out_shape=jax.ShapeDtypeStruct((bs, 1, SB_W), F32),
        compiler_params=_cparams(("parallel", "arbitrary")),
        name="sb_decode",
    )(page_table.reshape(-1), q.reshape(bs, 1, SB_W), bias.reshape(SB_HEADS, 1), _suffix_matrix(),
      *([cache_kt] * group), *([cache_vt] * group))
    return out.reshape(bs, SB_W)


def _ffn_decode_kernel(pt_ref, x_ref, g_ref, wi_ref, wo_ref, gf_ref, q_ref, bias_ref, u2_ref, *refs, n_pages, final, tf):
    k_refs, v_refs = refs[:n_pages], refs[n_pages:2 * n_pages]
    o_ref, od_ref, act_ref = refs[2 * n_pages:]
    zero = lambda w: jnp.zeros((SB_HEADS, w), F32)
    _, (_, acc) = _alternate(
        _ffn_stages(x_ref, g_ref, wi_ref, wo_ref, gf_ref, o_ref, act_ref, final=final, tf=tf),
        _sb_decode_stages(q_ref, bias_ref, u2_ref, k_refs, v_refs, zero(SB_TK), zero(SB_W), per_stage=2))
    od_ref[0] = jnp.sum(jnp.where(_sb_own_head(), acc, 0.0), axis=0, keepdims=True)


def ffn_decode(x, g, w_in, w_out, gf, q, cache_kt, cache_vt, page_table, bias, layer, *, tf, final):
    t, d = x.shape
    dff = w_out.shape[1]
    bs, n_pages = page_table.shape
    page = cache_kt.shape[3]
    tm = t // bs
    assert page == SB_TK and tm * bs == t and tm % 8 == 0

    def page_spec(g):
        return pl.BlockSpec((None, None, SB_W, page), lambda i, pt: (layer, pt[i * n_pages + g], 0, 0))

    grid_spec = pltpu.PrefetchScalarGridSpec(
        num_scalar_prefetch=1,
        grid=(bs,),
        in_specs=_ffn_in_specs(tm, d, dff, layer)
        + [pl.BlockSpec((1, 1, SB_W), lambda i, pt: (i, 0, 0)), _resident((SB_HEADS, 1)),
           _resident((2 * SB_TK, 2 * SB_TK))]
        + [page_spec(g) for g in range(n_pages)] * 2,
        out_specs=[pl.BlockSpec((tm, d), lambda i, pt: (i, 0)), pl.BlockSpec((1, 1, SB_W), lambda i, pt: (i, 0, 0))],
        scratch_shapes=[pltpu.VMEM((tm, dff), BF16)],
    )
    y, o = pl.pallas_call(
        functools.partial(_ffn_decode_kernel, n_pages=n_pages, final=final, tf=tf),
        grid_spec=grid_spec,
        out_shape=[jax.ShapeDtypeStruct((t, d), F32), jax.ShapeDtypeStruct((bs, 1, SB_W), F32)],
        compiler_params=_cparams(("parallel",)),
        name="ffn_decode",
    )(page_table.reshape(-1), x, g.reshape(1, d), w_in, w_out, gf.reshape(1, d), q.reshape(bs, 1, SB_W),
      bias.reshape(SB_HEADS, 1), _suffix_matrix(), *([cache_kt] * n_pages), *([cache_vt] * n_pages))
    return y, o.reshape(bs, SB_W)


def _head_indicator(width, head):
    i = np.arange(width) // head
    return jnp.asarray(i[:, None] == i[None, :], BF16)


def _gdn_qkv_post(conv_out, seg):
    act = conv_out * _sigmoid(conv_out)
    q = act[:, :GDN_QK_W]
    k = act[:, GDN_QK_W:2 * GDN_QK_W]
    v = act[:, 2 * GDN_QK_W:]
    qn = q * lax.rsqrt(_dot_exact_rhs(q * q, seg) + L2_EPS) * (GDN_DK ** -0.5)
    kn = k * lax.rsqrt(_dot_exact_rhs(k * k, seg) + L2_EPS)
    return qn, kn, v


def _gdn_gates(ab, alog, dtb):
    g = -jnp.exp(alog) * _softplus(ab + dtb)
    return g, _sigmoid(ab)


def _gdn_pre_kernel(x_ref, halo_ref, ab_ref, cw_ref, alog_ref, dtb_ref, seg_ref, q_ref, k_ref, v_ref, gb_ref):
    i = pl.program_id(1)
    x = x_ref[0]
    tl = x.shape[0]
    halo = jnp.where(i > 0, halo_ref[0], 0.0)
    nh = halo.shape[0]
    ext = jnp.concatenate([halo, x], axis=0)
    out = None
    for tap in range(GDN_CONV):
        d = GDN_CONV - 1 - tap
        src = x if d == 0 else pltpu.roll(ext, d, axis=0)[nh:]
        term = src * cw_ref[tap:tap + 1, :]
        out = term if out is None else out + term
    qn, kn, v = _gdn_qkv_post(out, seg_ref[...])
    q_ref[0] = qn
    k_ref[0] = kn
    v_ref[0] = v

    g, beta = _gdn_gates(ab_ref[0], alog_ref[...], dtb_ref[...])
    tt = lax.broadcasted_iota(jnp.int32, g.shape, 0) % GDN_CHUNK
    k = 1
    while k < GDN_CHUNK:
        g = g + jnp.where(tt >= k, pltpu.roll(g, k, axis=0), 0.0)
        k *= 2
    lane = lax.broadcasted_iota(jnp.int32, g.shape, 1)
    gb_ref[0] = jnp.where(lane < GDN_HEADS, g, beta)


def gdn_pre(p3, conv_w, alog, dtb, *, tl):
    b, l, _ = p3.shape
    nh = 8
    out_sd = jax.ShapeDtypeStruct((b, l, GDN_QK_W), F32)
    return pl.pallas_call(
        _gdn_pre_kernel,
        grid=(b, l // tl),
        in_specs=[pl.BlockSpec((1, tl, GDN_QKV_W), lambda n, i: (n, i, P_GQKV // GDN_QKV_W)),
                  pl.BlockSpec((1, nh, GDN_QKV_W), lambda n, i: (n, jnp.maximum(i * (tl // nh) - 1, 0), 0)),
                  pl.BlockSpec((1, tl, LANES), lambda n, i: (n, i, P_AB // LANES)),
                  pl.BlockSpec((GDN_CONV, GDN_QKV_W), lambda n, i: (0, 0)),
                  pl.BlockSpec((1, LANES), lambda n, i: (0, 0)),
                  pl.BlockSpec((1, LANES), lambda n, i: (0, 0)),
                  pl.BlockSpec((GDN_QK_W, GDN_QK_W), lambda n, i: (0, 0))],
        out_specs=[pl.BlockSpec((1, tl, GDN_QK_W), lambda n, i: (n, i, 0))] * 3
        + [pl.BlockSpec((1, tl, LANES), lambda n, i: (n, i, 0))],
        out_shape=[out_sd, out_sd, out_sd, jax.ShapeDtypeStruct((b, l, LANES), F32)],
        compiler_params=_cparams(("parallel", "parallel")),
        name="gdn_pre",
    )(p3, p3, p3, conv_w, alog, dtb, _head_indicator(GDN_QK_W, GDN_DK))


def _gate_broadcast_matrix():
    r = np.arange(2 * LANES)[:, None] % LANES
    c = np.arange(2 * LANES)[None, :]
    return jnp.asarray(np.where(c < LANES, r < GDN_HEADS, (r >= GDN_HEADS) & (r < 2 * GDN_HEADS)), BF16)


def _gdn_chunk_terms(q, k, v, gb, e_mat):
    ch = GDN_CHUNK
    r = GDN_HEADS * ch
    sh = ch.bit_length() - 1
    chunks = range(len(q))
    ri = lax.broadcasted_iota(jnp.int32, (r, r), 0)
    ci = lax.broadcasted_iota(jnp.int32, (r, r), 1)
    same_head = (ri >> sh) == (ci >> sh)
    incl = same_head & (ri >= ci)
    strict = same_head & (ri > ci)
    eye = (ri == ci).astype(F32)

    def stack(x):
        return jnp.where(same_head, jnp.concatenate([x] * GDN_HEADS, axis=0), 0.0)

    def widen(x):
        return jnp.concatenate([x] * (r // LANES), axis=1)

    kbd, qbd, vbd = [stack(x) for x in k], [stack(x) for x in q], [stack(x) for x in v]
    yield

    l1 = lax.broadcasted_iota(jnp.int32, (r, LANES), 1)
    r1 = lax.broadcasted_iota(jnp.int32, (r, LANES), 0) >> sh
    gate_lane = (l1 == r1) | (l1 == r1 + GDN_HEADS)
    gsel = [_split(jnp.where(gate_lane, jnp.concatenate([x] * GDN_HEADS, axis=0), 0.0)) for x in gb]
    gbc = [_dot(jnp.concatenate([hi, lo], axis=1), e_mat) for hi, lo in gsel]
    g1 = [x[:, :LANES] for x in gbc]
    gl1 = [jnp.concatenate([jnp.broadcast_to(x[(h + 1) * ch - 1:(h + 1) * ch, :], (ch, LANES))
                            for h in range(GDN_HEADS)], axis=0) for x in g1]
    gcol = [widen(x) for x in g1]
    bcol = [widen(x[:, LANES:]) for x in gbc]
    eg = [widen(jnp.exp(x)) for x in g1]
    egl = [widen(jnp.exp(x)) for x in gl1]
    ekd = [widen(jnp.exp(gl1[i] - g1[i])) for i in chunks]

    yield
    decay = [jnp.where(incl, jnp.exp(jnp.where(incl, x - x.T, 0.0)), 0.0) for x in gcol]
    kb = [kbd[i] * bcol[i] for i in chunks]
    kbd_b = [x.astype(BF16) for x in kbd]
    a_mat = [jnp.where(strict, _dot_nt(kb[i].astype(BF16), kbd_b[i]) * decay[i], 0.0) for i in chunks]
    yield

    t_inv = [eye - jnp.where((ri >> 1) == (ci >> 1), x, 0.0) for x in a_mat]
    for lv in range(1, sh):
        off = ((ri >> (lv + 1)) == (ci >> (lv + 1))) & ((ri >> lv) != (ci >> lv))
        tb = [x.astype(BF16) for x in t_inv]
        m = [_dot(jnp.where(off, a_mat[i], 0.0).astype(BF16), tb[i]).astype(BF16) for i in chunks]
        yield
        t_inv = [t_inv[i] - _dot(tb[i], m[i]) for i in chunks]
        yield
    res = [(eye - t_inv[i]) - _mm3(a_mat[i], t_inv[i]) for i in chunks]
    yield
    t_inv = [t_inv[i] + _dot(t_inv[i].astype(BF16), res[i].astype(BF16)) for i in chunks]
    t_split = [_split(x) for x in t_inv]
    yield

    def apply_t(i, x):
        th, tl = t_split[i]
        xh, xl = _split(x)
        return _dot(th, xh) + (_dot(th, xl) + _dot(tl, xh))

    u = [apply_t(i, vbd[i] * bcol[i]) for i in chunks]
    yield
    w = [apply_t(i, kb[i] * eg[i]).astype(BF16) for i in chunks]
    yield
    qk = [jnp.where(incl, _dot_nt(qbd[i].astype(BF16), kbd_b[i]) * decay[i], 0.0).astype(BF16) for i in chunks]
    qd = [(qbd[i] * eg[i]).astype(BF16) for i in chunks]
    kd = [(kbd[i] * ekd[i]).astype(BF16) for i in chunks]
    return list(zip(u, w, qk, qd, kd, egl))


def _gdn_chunk_kernel(q_ref, k_ref, v_ref, gb_ref, z_ref, gn_ref, e_ref, seg_ref, og_ref, s_ref, st_ref, *, nc):
    c = pl.program_id(1)
    ch = GDN_CHUNK

    @pl.when(c == 0)
    def _():
        st_ref[...] = jnp.zeros_like(st_ref)

    def terms_of(chunk_ids):
        rows = [slice(i * ch, (i + 1) * ch) for i in chunk_ids]
        return _gdn_chunk_terms([q_ref[0, x, :] for x in rows], [k_ref[0, x, :] for x in rows],
                                [v_ref[0, x, :] for x in rows], [gb_ref[0, x, :] for x in rows], e_ref[...])

    state = {"s": st_ref[...]}
    outs = []

    def recurrence(terms):
        for u, w, qk, qd, kd, egl in terms:
            sb = state["s"].astype(BF16)
            vnb = (u - _dot(w, sb)).astype(BF16)
            o = _dot(qd, sb)
            yield
            o = o + _dot(qk, vnb)
            state["s"] = state["s"] * egl + _dot_tn(kd, vnb)
            oc = o[0:ch]
            for h in range(1, GDN_HEADS):
                oc = oc + o[h * ch:(h + 1) * ch]
            outs.append(oc)
            yield

    half = max(nc // 2, 1)
    terms_a, = _alternate(terms_of(range(half)))
    if nc > half:
        terms_b, _ = _alternate(terms_of(range(half, nc)), recurrence(terms_a))
        _alternate(recurrence(terms_b))
    else:
        _alternate(recurrence(terms_a))
    s = state["s"]
    st_ref[...] = s

    oc = jnp.concatenate(outs, axis=0)
    ms = _dot_exact_rhs(oc * oc, seg_ref[...]) * (1.0 / GDN_DV)
    zz = z_ref[0]
    og_ref[0] = oc * lax.rsqrt(ms + RMS_EPS) * gn_ref[...] * (zz * _sigmoid(zz))

    @pl.when(c == pl.num_programs(1) - 1)
    def _():
        for h in range(GDN_HEADS):
            s_ref[0, h] = s[h * GDN_DK:(h + 1) * GDN_DK, h * GDN_DV:(h + 1) * GDN_DV]


def gdn_chunk(qn, kn, v, gb, p3, gnorm, *, nc):
    b, l, _ = qn.shape
    ch = GDN_CHUNK * nc
    r = GDN_HEADS * GDN_CHUNK
    blk = lambda w: pl.BlockSpec((1, ch, w), lambda n, c: (n, c, 0))
    return pl.pallas_call(
        functools.partial(_gdn_chunk_kernel, nc=nc),
        grid=(b, l // ch),
        in_specs=[blk(GDN_QK_W), blk(GDN_QK_W), blk(GDN_V_W), blk(LANES),
                  pl.BlockSpec((1, ch, GDN_V_W), lambda n, c: (n, c, P_GZ // GDN_V_W)),
                  pl.BlockSpec((1, GDN_V_W), lambda n, c: (0, 0)),
                  pl.BlockSpec((2 * LANES, 2 * LANES), lambda n, c: (0, 0)),
                  pl.BlockSpec((GDN_V_W, GDN_V_W), lambda n, c: (0, 0))],
        out_specs=[pl.BlockSpec((1, ch, GDN_V_W), lambda n, c: (n, c, 0)),
                   pl.BlockSpec((1, GDN_HEADS, GDN_DK, GDN_DV), lambda n, c: (n, 0, 0, 0))],
        out_shape=[jax.ShapeDtypeStruct((b, l, GDN_V_W), F32),
                   jax.ShapeDtypeStruct((b, GDN_HEADS, GDN_DK, GDN_DV), F32)],
        scratch_shapes=[pltpu.VMEM((r, r), F32)],
        compiler_params=_cparams(("parallel", "arbitrary")),
        name="gdn_chunk",
    )(qn, kn, v, gb, p3, jnp.tile(gnorm, GDN_HEADS).reshape(1, GDN_V_W), _gate_broadcast_matrix(),
      _head_indicator(GDN_V_W, GDN_DV))


def _sample_pre_kernel(up_ref, x_ref, ab_ref, z_ref, ph_ref, ch_ref, pw_ref, psc_ref, cw_ref, alog_ref, dtb_ref,
                       seg_ref, *refs, pos0, carried):
    op_ref, qt_ref, kt_ref, vt_ref, zt_ref, gbt_ref, pn_ref, cn_ref = refs[2 * carried:]
    u = up_ref[...]
    hist = ph_ref.shape[0]
    grp = lax.broadcasted_iota(jnp.int32, u.shape, 1) // POOL_GC
    tails, tail = {0: jnp.zeros_like(u)}, jnp.zeros_like(u)
    for n in range(1, hist + 1):
        tail = tail + ph_ref[hist - n]
        tails[n] = tail
    means = [(tails[min(win - 1, hist)] + u) / float(min(win, pos0 + 1)) for win in POOL_WINDOWS]
    pooled = _pool_select(grp, means) - u
    op_ref[...] = _dot(pooled.astype(BF16), pw_ref[...]) * psc_ref[...]
    pn_ref[0:hist - 1] = ph_ref[1:hist]
    pn_ref[hist - 1] = u

    x = x_ref[...]
    out = ch_ref[0] * cw_ref[0:1, :]
    for tap in range(1, GDN_CONV - 1):
        out = out + ch_ref[tap] * cw_ref[tap:tap + 1, :]
    out = out + x * cw_ref[GDN_CONV - 1:, :]
    cn_ref[0:GDN_CONV - 2] = ch_ref[1:GDN_CONV - 1]
    cn_ref[GDN_CONV - 2] = x
    qn, kn, v = _gdn_qkv_post(out, seg_ref[...])
    qt_ref[...] = qn.T
    kt_ref[...] = kn.T
    vt_ref[...] = v.T
    zt_ref[...] = z_ref[...].T

    g, beta = _gdn_gates(ab_ref[...], alog_ref[...], dtb_ref[...])
    lane = lax.broadcasted_iota(jnp.int32, g.shape, 1)
    gbt_ref[...] = jnp.where(lane < GDN_HEADS, jnp.exp(g), beta).T


def sample_pre(p, pool_hist, conv_hist, pool_new, conv_new, layer, pool_w_bd, pool_scale, conv_w, alog, dtb, *, pos0):
    bs = p.shape[0]
    depth = pool_hist.shape[0]
    full = lambda shape: pl.BlockSpec(shape, lambda i: (0,) * len(shape))
    of_layer = lambda shape: pl.BlockSpec((None, *shape), lambda i: (layer,) + (0,) * len(shape))
    sd = lambda *shape: jax.ShapeDtypeStruct(shape, F32)
    pool_rows, conv_rows = pool_hist.shape[1:], conv_hist.shape[1:]
    carried = pool_new is not None
    outs = pl.pallas_call(
        functools.partial(_sample_pre_kernel, pos0=pos0, carried=carried),
        grid=(1,),
        in_specs=[pl.BlockSpec((bs, POOL_W), lambda i: (0, P_POOL // POOL_W)),
                  pl.BlockSpec((bs, GDN_QKV_W), lambda i: (0, P_GQKV // GDN_QKV_W)),
                  pl.BlockSpec((bs, LANES), lambda i: (0, P_AB // LANES)),
                  pl.BlockSpec((bs, GDN_V_W), lambda i: (0, P_GZ // GDN_V_W)),
                  of_layer(pool_rows), of_layer(conv_rows), of_layer((POOL_W, POOL_W)), full((1, POOL_W)),
                  full((GDN_CONV, GDN_QKV_W)), full((1, LANES)), full((1, LANES)), full((GDN_QK_W, GDN_QK_W))]
        + [pl.BlockSpec(memory_space=pl.ANY)] * (2 * carried),
        out_specs=[full((bs, POOL_W)), full((GDN_QK_W, bs)), full((GDN_QK_W, bs)), full((GDN_V_W, bs)),
                   full((GDN_V_W, bs)), full((LANES, bs)), of_layer(pool_rows), of_layer(conv_rows)],
        out_shape=[sd(bs, POOL_W), sd(GDN_QK_W, bs), sd(GDN_QK_W, bs), sd(GDN_V_W, bs), sd(GDN_V_W, bs),
                   sd(LANES, bs), sd(depth, *pool_rows), sd(depth, *conv_rows)],
        input_output_aliases={12: 6, 13: 7} if carried else {},
        compiler_params=_cparams(("arbitrary",)),
        name="sample_pre",
    )(p, p, p, p, pool_hist, conv_hist, pool_w_bd, pool_scale.reshape(1, POOL_W), conv_w, alog, dtb,
      _head_indicator(GDN_QK_W, GDN_DK), *([pool_new, conv_new] if carried else []))
    return outs


def _gdn_step_kernel(s_ref, k_ref, q_ref, v_ref, z_ref, gb_ref, gn_ref, *refs):
    so_ref, og_ref = refs[-2:]
    h = pl.program_id(0)
    eg = gb_ref[pl.ds(h, 1), :]
    beta = gb_ref[pl.ds(h + GDN_HEADS, 1), :]
    ks = None
    for dk in range(GDN_DK):
        term = k_ref[dk:dk + 1, :] * (s_ref[dk] * eg)
        ks = term if ks is None else ks + term
    delta = (v_ref[...] - ks) * beta
    o = None
    for dk in range(GDN_DK):
        s_new = s_ref[dk] * eg + k_ref[dk:dk + 1, :] * delta
        so_ref[dk] = s_new
        term = q_ref[dk:dk + 1, :] * s_new
        o = term if o is None else o + term
    ms = jnp.mean(o * o, axis=0, keepdims=True)
    zz = z_ref[...]
    og_ref[...] = o * lax.rsqrt(ms + RMS_EPS) * gn_ref[...] * (zz * _sigmoid(zz))


def gdn_step(state, state_new, layer, kt, qt, vt, zt, gbt, gnorm):
    depth, h, dk, dv, bs = state.shape
    of_head = lambda rows: pl.BlockSpec((rows, bs), lambda i: (i, 0))
    s_spec = pl.BlockSpec((None, None, dk, dv, bs), lambda i: (layer, i, 0, 0, 0))
    carried = state_new is not None
    return pl.pallas_call(
        _gdn_step_kernel,
        grid=(h,),
        in_specs=[s_spec, of_head(dk), of_head(dk), of_head(dv), of_head(dv),
                  pl.BlockSpec((LANES, bs), lambda i: (0, 0)), pl.BlockSpec((dv, 1), lambda i: (0, 0))]
        + [pl.BlockSpec(memory_space=pl.ANY)] * carried,
        out_specs=[s_spec, of_head(dv)],
        out_shape=[jax.ShapeDtypeStruct(state.shape, F32), jax.ShapeDtypeStruct((h * dv, bs), F32)],
        input_output_aliases={7: 0} if carried else {},
        compiler_params=_cparams(("parallel",)),
        name="gdn_step",
    )(state, kt, qt, vt, zt, gbt, gnorm.reshape(dv, 1), *([state_new] if carried else []))


def _largest_tile(n, cap):
    t = min(n, cap)
    while n % t:
        t //= 2
    return t


def _pack_w_in(w_in):
    depth, d, _ = w_in.shape
    pad = jnp.zeros((depth, d, P_W - P_AB - (R_GZ - R_GA)), w_in.dtype)
    return jnp.concatenate([w_in[..., R_GQKV:R_GA], w_in[..., R_POOL:R_GQKV], w_in[..., R_GZ:R_GATE],
                            w_in[..., R_GA:R_GZ], pad], axis=-1).astype(BF16)


def _lane_pad(x):
    return jnp.pad(x, ((0, 0), (0, LANES - x.shape[1])))


def kernel(x_prompt, x_sample, cache_sb_k, cache_sb_v, page_table, state_pool, state_gdn_conv, state_gdn, norm1_g, w_in, pool_w, pool_scale, sb_bias, gdn_conv_w, gdn_a_log, gdn_dt_bias, gdn_norm_g, w_branch, w_o, norm2_g, w_ffn_in, w_ffn_out, normf_g):
    bp, seq, d = x_prompt.shape
    bs = x_sample.shape[0]
    depth = w_in.shape[0]
    n_phys, page = cache_sb_k.shape[1], cache_sb_k.shape[2]
    past_len = page_table.shape[1] * page
    tp = bp * seq

    w_in_p = _pack_w_in(w_in)
    w_gate_b = w_in[..., R_GATE:R_END].astype(BF16)
    w_branch_b = w_branch.astype(BF16)
    w_o_b = w_o.astype(BF16)
    w_ffn_in_b = w_ffn_in.astype(BF16)
    w_ffn_out_b = w_ffn_out.astype(BF16)
    eye_g = jnp.eye(len(POOL_WINDOWS), dtype=F32)
    pool_w_bd = jnp.einsum("lgcd,gh->lgchd", pool_w, eye_g).reshape(depth, POOL_W, POOL_W).astype(BF16)
    alog_p = _lane_pad(gdn_a_log)
    dtb_p = _lane_pad(gdn_dt_bias)
    cache_kt = jnp.transpose(cache_sb_k, (0, 1, 3, 4, 2)).reshape(depth, n_phys, SB_W, page)
    cache_vt = jnp.transpose(cache_sb_v, (0, 1, 3, 4, 2)).reshape(depth, n_phys, SB_W, page)
    n_pages = page_table.shape[1]
    page_group = _largest_tile(n_pages, 16)
    fuse_decode = page_group == n_pages and tp % bs == 0 and (tp // bs) % 8 == 0

    tm_p = _largest_tile(tp, 512)
    tm_merge = _largest_tile(tp, 512)
    tm_s = _largest_tile(bs, 128)
    tq = _largest_tile(seq, 256)
    tl = _largest_tile(seq, 256)

    xp = x_prompt.reshape(tp, d)
    xs = x_sample.reshape(bs, d)
    pool_hist = jnp.transpose(state_pool, (0, 2, 1, 3))
    conv_hist = jnp.transpose(state_gdn_conv, (0, 2, 1, 3))
    gdn_state = jnp.transpose(state_gdn, (0, 2, 3, 4, 1))
    pool_s = conv_s = gdn_s = None

    pool_p, conv_p, gdn_p, k_s, v_s = [], [], [], [], []
    kt_new = vt_new = None
    for l in range(depth):
        final = l == depth - 1
        p, kt_new, vt_new = rms_matmul(xp, norm1_g[l], w_in_p, l, tm=tm_p, tn=1024,
                                       kv_t=(depth, seq, kt_new, vt_new))
        p3 = p.reshape(bp, seq, P_W)
        o_pool = pool_prompt(p3, pool_w_bd[l], pool_scale[l])
        o_sb = sb_prompt(p3, sb_bias[l], tq=tq)
        qn, kn, gv, gb = gdn_pre(p3, gdn_conv_w[l], alog_p[l:l + 1], dtb_p[l:l + 1], tl=tl)
        o_g, s_p = gdn_chunk(qn, kn, gv, gb, p3, gdn_norm_g[l], nc=_largest_tile(seq // GDN_CHUNK, 8))
        xp = merge(xp, norm1_g[l], o_pool.reshape(tp, POOL_W), o_sb.reshape(tp, SB_W), o_g.reshape(tp, GDN_V_W),
                   w_gate_b, w_branch_b, w_o_b, l, tm=tm_merge)
        ps = rms_matmul(xs, norm1_g[l], w_in_p, l, tm=tm_s, tn=1024)
        if fuse_decode:
            xp, o_sb_s = ffn_decode(xp, norm2_g[l], w_ffn_in_b, w_ffn_out_b, normf_g, ps[:, P_SBQ:P_SBK],
                                    cache_kt, cache_vt, page_table, sb_bias[l], l, tf=256, final=final)
        else:
            xp = ffn(xp, norm2_g[l], w_ffn_in_b, w_ffn_out_b, normf_g, l, tm=tm_p, tf=256, final=final)
            o_sb_s = sb_decode(ps[:, P_SBQ:P_SBK], cache_kt, cache_vt, page_table, sb_bias[l], l, group=page_group)
        pool_p.append(p3[:, seq - POOL_HIST:, P_POOL:P_SBQ])
        conv_p.append(p3[:, seq - (GDN_CONV - 1):, P_GQKV:P_POOL])
        gdn_p.append(s_p)

        o_pool_s, qt_s, kt_s, vt_s, zt_s, gbt_s, pool_s, conv_s = sample_pre(
            ps, pool_hist, conv_hist, pool_s, conv_s, l, pool_w_bd, pool_scale[l], gdn_conv_w[l], alog_p[l:l + 1],
            dtb_p[l:l + 1], pos0=past_len)
        gdn_s, ogt_s = gdn_step(gdn_state, gdn_s, l, kt_s, qt_s, vt_s, zt_s, gbt_s, gdn_norm_g[l])
        xs = merge(xs, norm1_g[l], o_pool_s, o_sb_s, ogt_s.T, w_gate_b, w_branch_b, w_o_b, l, tm=tm_s)
        xs = ffn(xs, norm2_g[l], w_ffn_in_b, w_ffn_out_b, normf_g, l, tm=tm_s, tf=256, final=final)
        k_s.append(ps[:, P_SBK:P_SBV].reshape(bs, 1, SB_HEADS, SB_HEAD_DIM))
        v_s.append(ps[:, P_SBV:P_GZ].reshape(bs, 1, SB_HEADS, SB_HEAD_DIM))

    y_prompt = xp.reshape(bp, seq, d)
    y_sample = xs.reshape(bs, 1, d)
    new_kv = [jnp.transpose(t.reshape(depth, bp, SB_HEADS, SB_HEAD_DIM, seq), (0, 1, 4, 2, 3))
              for t in (kt_new, vt_new)]
    return (y_prompt, y_sample, *new_kv, jnp.stack(pool_p), jnp.stack(conv_p), jnp.stack(gdn_p),
            jnp.stack(k_s), jnp.stack(v_s), jnp.transpose(pool_s, (0, 2, 1, 3)), jnp.transpose(conv_s, (0, 2, 1, 3)),
            jnp.transpose(gdn_s, (0, 4, 1, 2, 3)))
```
